```python
import jax, jax.numpy as jnp
from jax import lax
import numpy as np

D_MODEL = 4096
BATCH = 2
SEQ = 8192
DEPTH = 2

N_MIXERS = 2
N_CONV_LAYERS = (DEPTH + N_MIXERS - 1) // N_MIXERS
N_ATTN_LAYERS = DEPTH // N_MIXERS
N_META = 16
GRID_W = 64
D_FF = 11008
N_HEADS = 32
N_KV_HEADS = 8
HEAD_DIM = D_MODEL // N_HEADS
GROUP = N_HEADS // N_KV_HEADS
ROPE_AXIS_DIM = HEAD_DIM // 2
ROPE_THETA = 10000.0
Q_BLOCK = 128
CONV_WIDTH = 3
NORM_EPS = 1e-6
MACARON_WEIGHT = 0.5

kernel_name = "hybrid_conv_axial_gqa_macaron_encoder"


def _rmsnorm(x, g):
    xf = x.astype(jnp.float32)
    y = xf * lax.rsqrt(jnp.mean(xf * xf, axis=-1, keepdims=True) + NORM_EPS)
    return (y * g.astype(jnp.float32)).astype(x.dtype)


def _swiglu(x, w_gate, w_up, w_down):
    return (jax.nn.silu(x @ w_gate) * (x @ w_up)) @ w_down


def _short_conv_mixer(x, w_in, conv_w, conv_b, w_out):
    b_gate, c_gate, h = jnp.split(x @ w_in, 3, axis=-1)
    u = c_gate * h
    n = u.shape[1]
    up = jnp.pad(u, ((0, 0), (1, 1), (0, 0)))
    conv = (up[:, 0:n] * conv_w[0] + up[:, 1:n + 1] * conv_w[1]
            + up[:, 2:n + 2] * conv_w[2] + conv_b)
    return (b_gate * conv) @ w_out


def _axial_angles(n_real):
    rows = n_real // GRID_W
    real_row = jnp.repeat(jnp.arange(rows, dtype=jnp.float32), GRID_W)
    real_col = jnp.tile(jnp.arange(GRID_W, dtype=jnp.float32), rows)
    row = jnp.concatenate([jnp.full((N_META,), -1.0, jnp.float32), real_row])
    col = jnp.concatenate([jnp.arange(N_META, dtype=jnp.float32), real_col])
    inv_freq = ROPE_THETA ** (-jnp.arange(0, ROPE_AXIS_DIM, 2, dtype=jnp.float32) / ROPE_AXIS_DIM)
    return row[:, None] * inv_freq, col[:, None] * inv_freq


def _rope_axis(t, ang):
    half = t.shape[-1] // 2
    t1, t2 = t[..., :half], t[..., half:]
    cos = jnp.cos(ang)[None, :, None, :]
    sin = jnp.sin(ang)[None, :, None, :]
    return jnp.concatenate([t1 * cos - t2 * sin, t2 * cos + t1 * sin], axis=-1)


def _qk_prep(t, gain, row_ang, col_ang):
    tf = t.astype(jnp.float32)
    tf = tf * lax.rsqrt(jnp.mean(tf * tf, axis=-1, keepdims=True) + NORM_EPS) * gain.astype(jnp.float32)
    tf = jnp.concatenate([_rope_axis(tf[..., :ROPE_AXIS_DIM], row_ang),
                          _rope_axis(tf[..., ROPE_AXIS_DIM:], col_ang)], axis=-1)
    return tf.astype(t.dtype)


def _attend_block(qb, k, v):
    s = jnp.einsum('bqkgd,bskd->bkgqs', qb, k).astype(jnp.float32) * (HEAD_DIM ** -0.5)
    p = jax.nn.softmax(s, axis=-1).astype(v.dtype)
    return jnp.einsum('bkgqs,bskd->bqkgd', p, v)


def _axial_gqa_attention(x, w_qkv, q_gain, k_gain, w_o, row_ang, col_ang):
    bsz, n, _ = x.shape
    n_real = n - N_META
    q, k, v = jnp.split(x @ w_qkv, [N_HEADS * HEAD_DIM, (N_HEADS + N_KV_HEADS) * HEAD_DIM], axis=-1)
    q = _qk_prep(q.reshape(bsz, n, N_HEADS, HEAD_DIM), q_gain, row_ang, col_ang)
    k = _qk_prep(k.reshape(bsz, n, N_KV_HEADS, HEAD_DIM), k_gain, row_ang, col_ang)
    v = v.reshape(bsz, n, N_KV_HEADS, HEAD_DIM)
    q = q.reshape(bsz, n, N_KV_HEADS, GROUP, HEAD_DIM)
    attend = lambda qb: _attend_block(qb, k, v)
    out_meta = attend(q[:, :N_META])
    q_real = q[:, N_META:].reshape(bsz, n_real // Q_BLOCK, Q_BLOCK, N_KV_HEADS, GROUP, HEAD_DIM)
    out_real = lax.map(attend, jnp.swapaxes(q_real, 0, 1))
    out_real = jnp.swapaxes(out_real, 0, 1).reshape(bsz, n_real, N_KV_HEADS, GROUP, HEAD_DIM)
    out = jnp.concatenate([out_meta, out_real], axis=1).reshape(bsz, n, N_HEADS * HEAD_DIM)
    return out @ w_o


def setup_inputs(seed: int = 0) -> dict:
    key = jax.random.key(seed)
    ks = iter(jax.random.split(key, 32))
    f32 = jnp.float32

    def w(shape, fan_in):
        return jax.random.normal(next(ks), shape, f32) * (fan_in ** -0.5)

    def gain(shape):
        return 1.0 + 0.05 * jax.random.normal(next(ks), shape, f32)

    qkv_width = (N_HEADS + 2 * N_KV_HEADS) * HEAD_DIM
    return {
        "x": jax.random.normal(next(ks), (BATCH, SEQ, D_MODEL), f32),
        "meta_tokens": jax.random.normal(next(ks), (N_META, D_MODEL), f32),
        "ffn_a_norm": gain((DEPTH, D_MODEL)),
        "ffn_a_w_gate": w((DEPTH, D_MODEL, D_FF), D_MODEL),
        "ffn_a_w_up": w((DEPTH, D_MODEL, D_FF), D_MODEL),
        "ffn_a_w_down": w((DEPTH, D_FF, D_MODEL), D_FF),
        "ffn_b_norm": gain((DEPTH, D_MODEL)),
        "ffn_b_w_gate": w((DEPTH, D_MODEL, D_FF), D_MODEL),
        "ffn_b_w_up": w((DEPTH, D_MODEL, D_FF), D_MODEL),
        "ffn_b_w_down": w((DEPTH, D_FF, D_MODEL), D_FF),
        "conv_norm": gain((N_CONV_LAYERS, D_MODEL)),
        "conv_w_in": w((N_CONV_LAYERS, D_MODEL, 3 * D_MODEL), D_MODEL),
        "conv_w": w((N_CONV_LAYERS, CONV_WIDTH, D_MODEL), CONV_WIDTH),
        "conv_b": 0.02 * jax.random.normal(next(ks), (N_CONV_LAYERS, D_MODEL), f32),
        "conv_w_out": w((N_CONV_LAYERS, D_MODEL, D_MODEL), D_MODEL),
        "attn_norm": gain((N_ATTN_LAYERS, D_MODEL)),
        "attn_w_qkv": w((N_ATTN_LAYERS, D_MODEL, qkv_width), D_MODEL),
        "attn_q_norm": gain((N_ATTN_LAYERS, HEAD_DIM)),
        "attn_k_norm": gain((N_ATTN_LAYERS, HEAD_DIM)),
        "attn_w_o": w((N_ATTN_LAYERS, N_HEADS * HEAD_DIM, D_MODEL), N_HEADS * HEAD_DIM),
        "final_norm": gain((D_MODEL,)),
    }


def reference(x, meta_tokens, ffn_a_norm, ffn_a_w_gate, ffn_a_w_up, ffn_a_w_down,
              ffn_b_norm, ffn_b_w_gate, ffn_b_w_up, ffn_b_w_down,
              conv_norm, conv_w_in, conv_w, conv_b, conv_w_out,
              attn_norm, attn_w_qkv, attn_q_norm, attn_k_norm, attn_w_o,
              final_norm):
    bsz, n_real, d = x.shape
    meta = jnp.broadcast_to(meta_tokens.astype(x.dtype)[None], (bsz, N_META, d))
    h = jnp.concatenate([meta, x], axis=1)
    row_ang, col_ang = _axial_angles(n_real)

    for i in range(DEPTH):
        h = h + MACARON_WEIGHT * _swiglu(_rmsnorm(h, ffn_a_norm[i]),
                                         ffn_a_w_gate[i], ffn_a_w_up[i], ffn_a_w_down[i])
        j = i // N_MIXERS
        if i % N_MIXERS == 0:
            h = h + _short_conv_mixer(_rmsnorm(h, conv_norm[j]),
                                      conv_w_in[j], conv_w[j], conv_b[j], conv_w_out[j])
        else:
            h = h + _axial_gqa_attention(_rmsnorm(h, attn_norm[j]), attn_w_qkv[j],
                                         attn_q_norm[j], attn_k_norm[j], attn_w_o[j],
                                         row_ang, col_ang)
        h = h + MACARON_WEIGHT * _swiglu(_rmsnorm(h, ffn_b_norm[i]),
                                         ffn_b_w_gate[i], ffn_b_w_up[i], ffn_b_w_down[i])

    h = _rmsnorm(h, final_norm)
    return h[:, N_META:]
```

```python
import functools

import jax
import jax.numpy as jnp
from jax import lax
from jax.experimental import pallas as pl
from jax.experimental.pallas import tpu as pltpu

D_MODEL = 4096
BATCH = 2
SEQ = 8192
DEPTH = 2
N_META = 16
GRID_W = 64
D_FF = 11008
N_HEADS = 32
N_KV_HEADS = 8
HEAD_DIM = 128
GROUP = N_HEADS // N_KV_HEADS
ROPE_AXIS_DIM = HEAD_DIM // 2
ROPE_THETA = 10000.0
NORM_EPS = 1e-6
MACARON_WEIGHT = 0.5

M_REAL = BATCH * SEQ
M_META = BATCH * N_META
M_ROWS = M_REAL + M_META
Q_WIDTH = N_HEADS * HEAD_DIM
KV_WIDTH = N_KV_HEADS * HEAD_DIM
QKV_WIDTH = Q_WIDTH + 2 * KV_WIDTH

VMEM_LIMIT_BYTES = 56 * 1024 * 1024
SUBLANES = 8

ROWS_LARGE = 1824
ROWS_MEDIUM = 912
ROWS_SMALL = 304
FINAL_ROWS = 256
ATTN_Q_ROWS = 256
FFN_COLS = 256
PROJ_COLS = 512

F32 = jnp.float32
BF16 = jnp.bfloat16


def _params(semantics):
    return pltpu.CompilerParams(dimension_semantics=semantics,
                                vmem_limit_bytes=VMEM_LIMIT_BYTES)


def _rmsnorm_kernel(h_ref, g_ref, o_ref):
    x = h_ref[...]
    ms = jnp.mean(x * x, axis=-1, keepdims=True)
    o_ref[...] = (x * lax.rsqrt(ms + NORM_EPS) * g_ref[...]).astype(o_ref.dtype)


def _rmsnorm(h, gain, *, rows, out_dtype, block_rows, name):
    return pl.pallas_call(
        _rmsnorm_kernel,
        grid=(rows // block_rows,),
        in_specs=[pl.BlockSpec((block_rows, D_MODEL), lambda i: (i, 0)),
                  pl.BlockSpec((1, D_MODEL), lambda i: (0, 0))],
        out_specs=pl.BlockSpec((block_rows, D_MODEL), lambda i: (i, 0)),
        out_shape=jax.ShapeDtypeStruct((rows, D_MODEL), out_dtype),
        compiler_params=_params(("parallel",)),
        name=name,
    )(h, gain.reshape(1, D_MODEL))


def _mm_kernel(x_ref, w_ref, o_ref):
    o_ref[...] = jnp.dot(x_ref[...], w_ref[...],
                         preferred_element_type=F32).astype(o_ref.dtype)


def _mm_res_kernel(x_ref, w_ref, h_ref, o_ref, *, scale):
    y = jnp.dot(x_ref[...], w_ref[...], preferred_element_type=F32)
    o_ref[...] = h_ref[...] + scale * y


def _matmul(x, w, *, tm, tn, out_dtype, name, residual=None, scale=1.0):
    m, k = x.shape
    n = w.shape[1]
    x_spec = pl.BlockSpec((tm, k), lambda i, j: (i, 0),
                          pipeline_mode=pl.Buffered(1))
    w_spec = pl.BlockSpec((k, tn), lambda i, j: (0, j))
    o_spec = pl.BlockSpec((tm, tn), lambda i, j: (i, j))
    if residual is None:
        kernel, in_specs, args = _mm_kernel, [x_spec, w_spec], (x, w)
    else:
        kernel = functools.partial(_mm_res_kernel, scale=scale)
        in_specs, args = [x_spec, w_spec, o_spec], (x, w, residual)
    return pl.pallas_call(
        kernel,
        grid=(m // tm, n // tn),
        in_specs=in_specs,
        out_specs=o_spec,
        out_shape=jax.ShapeDtypeStruct((m, n), out_dtype),
        compiler_params=_params(("parallel", "arbitrary")),
        name=name,
    )(*args)


def _gate_up_kernel(x_ref, wg_ref, wu_ref, o_ref):
    x = x_ref[...]
    g = jnp.dot(x, wg_ref[...], preferred_element_type=F32)
    u = jnp.dot(x, wu_ref[...], preferred_element_type=F32)
    o_ref[...] = (g * jax.nn.sigmoid(g) * u).astype(o_ref.dtype)


def _gate_up(x, wg, wu, *, tm, tn, name):
    m, k = x.shape
    n = wg.shape[1]
    w_spec = pl.BlockSpec((k, tn), lambda i, j: (0, j))
    return pl.pallas_call(
        _gate_up_kernel,
        grid=(m // tm, n // tn),
        in_specs=[pl.BlockSpec((tm, k), lambda i, j: (i, 0),
                               pipeline_mode=pl.Buffered(1)),
                  w_spec, w_spec],
        out_specs=pl.BlockSpec((tm, tn), lambda i, j: (i, j)),
        out_shape=jax.ShapeDtypeStruct((m, n), BF16),
        compiler_params=_params(("parallel", "arbitrary")),
        name=name,
    )(x, wg, wu)


def _ffn(h, gain, wg, wu, wd, *, name):
    xn = _rmsnorm(h, gain, rows=M_ROWS, out_dtype=BF16, block_rows=ROWS_SMALL,
                  name=name + "_norm")
    act = _gate_up(xn, wg, wu, tm=ROWS_LARGE, tn=FFN_COLS, name=name + "_gate_up")
    return _matmul(act, wd, tm=ROWS_MEDIUM, tn=FFN_COLS, out_dtype=F32, residual=h,
                   scale=MACARON_WEIGHT, name=name + "_down")


CONV_ROWS = 512
CONV_COLS = 512
CONV_COL_BLOCKS = D_MODEL // CONV_COLS
CONV_TILES_PER_BATCH = SEQ // CONV_ROWS


def _conv_taps(u, u_left, u_right, b_gate, w_ref, bias_ref, o_ref):
    conv = (u_left * w_ref[0:1, :] + u * w_ref[1:2, :] + u_right * w_ref[2:3, :]
            + bias_ref[...])
    o_ref[...] = (b_gate * conv).astype(o_ref.dtype)


def _conv_real_kernel(b_ref, c_ref, h_ref, cp_ref, hp_ref, cn_ref, hn_ref,
                      w_ref, bias_ref, o_ref):
    rows = c_ref.shape[0]
    u = c_ref[...] * h_ref[...]
    u_prev = cp_ref[SUBLANES - 1:SUBLANES, :] * hp_ref[SUBLANES - 1:SUBLANES, :]
    last_in_batch = (pl.program_id(0) % CONV_TILES_PER_BATCH
                     == CONV_TILES_PER_BATCH - 1)
    u_next = jnp.where(last_in_batch, 0.0, cn_ref[0:1, :] * hn_ref[0:1, :])
    row = lax.broadcasted_iota(jnp.int32, u.shape, 0)
    u_left = jnp.where(row == 0, u_prev, pltpu.roll(u, 1, 0))
    u_right = jnp.where(row == rows - 1, u_next, pltpu.roll(u, rows - 1, 0))
    _conv_taps(u, u_left, u_right, b_ref[...], w_ref, bias_ref, o_ref)


def _conv_meta_kernel(b_ref, c_ref, h_ref, cn_ref, hn_ref, w_ref, bias_ref,
                      y_hbm_ref, o_ref):
    del y_hbm_ref
    u = c_ref[...] * h_ref[...]
    u_next = cn_ref[0:1, :] * hn_ref[0:1, :]
    row = lax.broadcasted_iota(jnp.int32, u.shape, 0)
    u_left = jnp.where(row == 0, 0.0, pltpu.roll(u, 1, 0))
    u_right = jnp.where(row == N_META - 1, u_next, pltpu.roll(u, N_META - 1, 0))
    _conv_taps(u, u_left, u_right, b_ref[...], w_ref, bias_ref, o_ref)


def _conv_gate(z, conv_w, conv_b):
    nb = CONV_COL_BLOCKS
    halo_per_tile = CONV_ROWS // SUBLANES
    meta_halo0 = (M_REAL + N_META - SUBLANES) // SUBLANES
    meta_halo_step = N_META // SUBLANES

    def prev_block(i):
        first = i % CONV_TILES_PER_BATCH == 0
        batch = i // CONV_TILES_PER_BATCH
        return jnp.where(first, meta_halo0 + meta_halo_step * batch,
                         i * halo_per_tile - 1)

    def next_block(i):
        last = i % CONV_TILES_PER_BATCH == CONV_TILES_PER_BATCH - 1
        return jnp.where(last, i * halo_per_tile, (i + 1) * halo_per_tile)

    def tile(col0):
        return pl.BlockSpec((CONV_ROWS, CONV_COLS), lambda i, j: (i, j + col0))

    def halo(block_fn, col0):
        return pl.BlockSpec((SUBLANES, CONV_COLS),
                            lambda i, j: (block_fn(i), j + col0))

    w_spec = pl.BlockSpec((3, CONV_COLS), lambda i, j: (0, j))
    bias_spec = pl.BlockSpec((1, CONV_COLS), lambda i, j: (0, j))
    bias = conv_b.reshape(1, D_MODEL)
    y = pl.pallas_call(
        _conv_real_kernel,
        grid=(M_REAL // CONV_ROWS, nb),
        in_specs=[tile(0), tile(nb), tile(2 * nb),
                  halo(prev_block, nb), halo(prev_block, 2 * nb),
                  halo(next_block, nb), halo(next_block, 2 * nb),
                  w_spec, bias_spec],
        out_specs=pl.BlockSpec((CONV_ROWS, CONV_COLS), lambda i, j: (i, j)),
        out_shape=jax.ShapeDtypeStruct((M_ROWS, D_MODEL), BF16),
        compiler_params=_params(("parallel", "parallel")),
        name="conv_real",
    )(z, z, z, z, z, z, z, conv_w, bias)

    meta0 = M_REAL // N_META
    real0_step = SEQ // SUBLANES

    def meta_tile(col0):
        return pl.BlockSpec((N_META, CONV_COLS), lambda b, j: (meta0 + b, j + col0))

    def first_real(col0):
        return pl.BlockSpec((SUBLANES, CONV_COLS),
                            lambda b, j: (b * real0_step, j + col0))

    w_spec = pl.BlockSpec((3, CONV_COLS), lambda b, j: (0, j))
    bias_spec = pl.BlockSpec((1, CONV_COLS), lambda b, j: (0, j))
    return pl.pallas_call(
        _conv_meta_kernel,
        grid=(BATCH, nb),
        in_specs=[meta_tile(0), meta_tile(nb), meta_tile(2 * nb),
                  first_real(nb), first_real(2 * nb), w_spec, bias_spec,
                  pl.BlockSpec(memory_space=pl.ANY)],
        out_specs=pl.BlockSpec((N_META, CONV_COLS), lambda b, j: (meta0 + b, j)),
        out_shape=jax.ShapeDtypeStruct((M_ROWS, D_MODEL), BF16),
        input_output_aliases={7: 0},
        compiler_params=_params(("parallel", "parallel")),
        name="conv_meta",
    )(z, z, z, z, z, conv_w, bias, y)


def _conv_mixer(h, gain, w_in, conv_w, conv_b, w_out):
    xn = _rmsnorm(h, gain, rows=M_ROWS, out_dtype=BF16, block_rows=ROWS_SMALL,
                  name="conv_norm")
    z = _matmul(xn, w_in, tm=ROWS_LARGE, tn=PROJ_COLS, out_dtype=F32, name="conv_in")
    y = _conv_gate(z, conv_w, conv_b)
    return _matmul(y, w_out, tm=ROWS_LARGE, tn=PROJ_COLS, out_dtype=F32, residual=h,
                   name="conv_out")


def _rope_tables():
    t = jnp.arange(SEQ, dtype=jnp.int32)
    real_row = (t // GRID_W).astype(F32)
    real_col = (t % GRID_W).astype(F32)
    meta_row = jnp.full((N_META,), -1.0, F32)
    meta_col = jnp.arange(N_META, dtype=F32)
    row = jnp.concatenate([jnp.tile(real_row, BATCH), jnp.tile(meta_row, BATCH)])
    col = jnp.concatenate([jnp.tile(real_col, BATCH), jnp.tile(meta_col, BATCH)])
    inv_freq = ROPE_THETA ** (-jnp.arange(0, ROPE_AXIS_DIM, 2, dtype=F32)
                              / ROPE_AXIS_DIM)
    ar = row[:, None] * inv_freq
    ac = col[:, None] * inv_freq
    zero = jnp.zeros_like(ar)
    cos = jnp.concatenate([jnp.cos(ar), jnp.cos(ar), jnp.cos(ac), jnp.cos(ac)], -1)
    sin_lo = jnp.concatenate([-jnp.sin(ar), zero, -jnp.sin(ac), zero], -1)
    sin_hi = jnp.concatenate([zero, jnp.sin(ar), zero, jnp.sin(ac)], -1)
    return cos, sin_lo, sin_hi


def _qk_prep_kernel(qkv_ref, cos_ref, slo_ref, shi_ref, qg_ref, kg_ref,
                    q_ref, k_ref, v_ref):
    cos = cos_ref[...]
    slo = slo_ref[...]
    shi = shi_ref[...]
    half = ROPE_AXIS_DIM // 2

    def prep(x, gain):
        ms = jnp.mean(x * x, axis=-1, keepdims=True)
        y = x * lax.rsqrt(ms + NORM_EPS) * gain
        return (y * cos + pltpu.roll(y, HEAD_DIM - half, 1) * slo
                + pltpu.roll(y, half, 1) * shi)

    q_gain = qg_ref[...] * (HEAD_DIM ** -0.5)
    k_gain = kg_ref[...]
    for hd in range(N_HEADS):
        sl = slice(hd * HEAD_DIM, (hd + 1) * HEAD_DIM)
        q_ref[:, sl] = prep(qkv_ref[:, sl], q_gain).astype(q_ref.dtype)
    for hd in range(N_KV_HEADS):
        sl = slice(hd * HEAD_DIM, (hd + 1) * HEAD_DIM)
        src = slice(Q_WIDTH + hd * HEAD_DIM, Q_WIDTH + (hd + 1) * HEAD_DIM)
        k_ref[:, sl] = prep(qkv_ref[:, src], k_gain).astype(k_ref.dtype)
    v_ref[...] = qkv_ref[:, Q_WIDTH + KV_WIDTH:].astype(v_ref.dtype)


def _qk_prep(qkv, q_gain, k_gain, *, block_rows):
    cos, slo, shi = _rope_tables()
    rows = lambda w: pl.BlockSpec((block_rows, w), lambda i: (i, 0))
    gain = pl.BlockSpec((1, HEAD_DIM), lambda i: (0, 0))
    return pl.pallas_call(
        _qk_prep_kernel,
        grid=(M_ROWS // block_rows,),
        in_specs=[rows(QKV_WIDTH), rows(HEAD_DIM), rows(HEAD_DIM), rows(HEAD_DIM),
                  gain, gain],
        out_specs=[rows(Q_WIDTH), rows(KV_WIDTH), rows(KV_WIDTH)],
        out_shape=[jax.ShapeDtypeStruct((M_ROWS, Q_WIDTH), BF16),
                   jax.ShapeDtypeStruct((M_ROWS, KV_WIDTH), BF16),
                   jax.ShapeDtypeStruct((M_ROWS, KV_WIDTH), BF16)],
        compiler_params=_params(("parallel",)),
        name="qk_prep",
    )(qkv, cos, slo, shi, q_gain.reshape(1, HEAD_DIM), k_gain.reshape(1, HEAD_DIM))


ATTN_KV_CHUNK = 512
META_PAD = 128

_NT = (((1,), (1,)), ((), ()))


def _flash_kernel(q_ref, k_ref, v_ref, km_ref, vm_ref, *rest, tq):
    o_ref = rest[-1]
    q = jnp.concatenate(
        [q_ref[:, g * HEAD_DIM:(g + 1) * HEAD_DIM] for g in range(GROUP)], axis=0)

    pad = jnp.zeros((META_PAD - N_META, HEAD_DIM), BF16)
    k_meta = jnp.concatenate([km_ref[...], pad], axis=0)
    v_meta = jnp.concatenate([vm_ref[...], pad], axis=0)
    s = lax.dot_general(q, k_meta, _NT, preferred_element_type=F32)
    col = lax.broadcasted_iota(jnp.int32, s.shape, 1)
    s = jnp.where(col < N_META, s, -jnp.inf)
    m0 = jnp.max(s, axis=-1, keepdims=True)
    p = jnp.exp(s - m0)
    l0 = jnp.sum(p, axis=-1, keepdims=True)
    acc0 = jnp.dot(p.astype(BF16), v_meta, preferred_element_type=F32)

    def body(c, carry):
        m, l, acc = carry
        off = pl.multiple_of(c * ATTN_KV_CHUNK, ATTN_KV_CHUNK)
        kc = k_ref[pl.ds(off, ATTN_KV_CHUNK), :]
        vc = v_ref[pl.ds(off, ATTN_KV_CHUNK), :]
        s = lax.dot_general(q, kc, _NT, preferred_element_type=F32)
        m_new = jnp.maximum(m, jnp.max(s, axis=-1, keepdims=True))
        alpha = jnp.exp(m - m_new)
        p = jnp.exp(s - m_new)
        l = alpha * l + jnp.sum(p, axis=-1, keepdims=True)
        acc = alpha * acc + jnp.dot(p.astype(BF16), vc, preferred_element_type=F32)
        return m_new, l, acc

    _, l, acc = lax.fori_loop(0, SEQ // ATTN_KV_CHUNK, body, (m0, l0, acc0))
    out = acc / l
    for g in range(GROUP):
        o_ref[:, g * HEAD_DIM:(g + 1) * HEAD_DIM] = (
            out[g * tq:(g + 1) * tq].astype(o_ref.dtype))


def _attention(q, k, v, *, tq):
    qw = GROUP * HEAD_DIM
    n_q = SEQ // tq
    meta0 = M_REAL // N_META
    kv_real = pl.BlockSpec((SEQ, HEAD_DIM), lambda b, g, i: (b, g))
    kv_meta = pl.BlockSpec((N_META, HEAD_DIM), lambda b, g, i: (meta0 + b, g))
    q_spec = pl.BlockSpec((tq, qw), lambda b, g, i: (b * n_q + i, g))
    o_real = pl.pallas_call(
        functools.partial(_flash_kernel, tq=tq),
        grid=(BATCH, N_KV_HEADS, n_q),
        in_specs=[q_spec, kv_real, kv_real, kv_meta, kv_meta],
        out_specs=q_spec,
        out_shape=jax.ShapeDtypeStruct((M_ROWS, Q_WIDTH), BF16),
        compiler_params=_params(("parallel", "parallel", "arbitrary")),
        name="attn_real",
    )(q, k, v, k, v)

    qm_spec = pl.BlockSpec((N_META, qw), lambda b, g, i: (meta0 + b, g))
    return pl.pallas_call(
        functools.partial(_flash_kernel, tq=N_META),
        grid=(BATCH, N_KV_HEADS, 1),
        in_specs=[qm_spec, kv_real, kv_real, kv_meta, kv_meta,
                  pl.BlockSpec(memory_space=pl.ANY)],
        out_specs=qm_spec,
        out_shape=jax.ShapeDtypeStruct((M_ROWS, Q_WIDTH), BF16),
        input_output_aliases={5: 0},
        compiler_params=_params(("parallel", "parallel", "arbitrary")),
        name="attn_meta",
    )(q, k, v, k, v, o_real)


def _attn_mixer(h, gain, w_qkv, q_gain, k_gain, w_o):
    xn = _rmsnorm(h, gain, rows=M_ROWS, out_dtype=BF16, block_rows=ROWS_SMALL,
                  name="attn_norm")
    qkv = _matmul(xn, w_qkv, tm=ROWS_LARGE, tn=PROJ_COLS, out_dtype=F32, name="attn_qkv")
    q, k, v = _qk_prep(qkv, q_gain, k_gain, block_rows=ROWS_SMALL)
    o = _attention(q, k, v, tq=ATTN_Q_ROWS)
    return _matmul(o, w_o, tm=ROWS_LARGE, tn=PROJ_COLS, out_dtype=F32, residual=h,
                   name="attn_out")


def kernel(x, meta_tokens, ffn_a_norm, ffn_a_w_gate, ffn_a_w_up, ffn_a_w_down, ffn_b_norm, ffn_b_w_gate, ffn_b_w_up, ffn_b_w_down, conv_norm, conv_w_in, conv_w, conv_b, conv_w_out, attn_norm, attn_w_qkv, attn_q_norm, attn_k_norm, attn_w_o, final_norm):
    bsz, n_real, d = x.shape
    assert (bsz, n_real, d) == (BATCH, SEQ, D_MODEL)
    meta = meta_tokens.astype(x.dtype)
    h = jnp.concatenate([x.reshape(M_REAL, D_MODEL)] + [meta] * BATCH, axis=0)
    bf = lambda w: w.astype(BF16)

    for i in range(DEPTH):
        h = _ffn(h, ffn_a_norm[i], bf(ffn_a_w_gate[i]), bf(ffn_a_w_up[i]),
                 bf(ffn_a_w_down[i]), name=f"ffn_a{i}")
        j = i // 2
        if i % 2 == 0:
            h = _conv_mixer(h, conv_norm[j], bf(conv_w_in[j]), conv_w[j],
                            conv_b[j], bf(conv_w_out[j]))
        else:
            h = _attn_mixer(h, attn_norm[j], bf(attn_w_qkv[j]), attn_q_norm[j],
                            attn_k_norm[j], bf(attn_w_o[j]))
        h = _ffn(h, ffn_b_norm[i], bf(ffn_b_w_gate[i]), bf(ffn_b_w_up[i]),
                 bf(ffn_b_w_down[i]), name=f"ffn_b{i}")

    out = _rmsnorm(h, final_norm, rows=M_REAL, out_dtype=F32, block_rows=FINAL_ROWS,
                   name="final_norm")
    return out.reshape(BATCH, SEQ, D_MODEL)
```

```python
import functools

import jax
import jax.numpy as jnp
from jax import lax
from jax.experimental import pallas as pl
from jax.experimental.pallas import tpu as pltpu

D_MODEL = 4096
BATCH = 2
SEQ = 8192
DEPTH = 2
N_META = 16
GRID_W = 64
D_FF = 11008
N_HEADS = 32
N_KV_HEADS = 8
HEAD_DIM = 128
GROUP = N_HEADS // N_KV_HEADS
ROPE_AXIS_DIM = HEAD_DIM // 2
ROPE_THETA = 10000.0
NORM_EPS = 1e-6
LOG2_E = 1.4426950408889634
MACARON_WEIGHT = 0.5

M_REAL = BATCH * SEQ
M_META = BATCH * N_META
M_ROWS = M_REAL + M_META
Q_WIDTH = N_HEADS * HEAD_DIM
KV_WIDTH = N_KV_HEADS * HEAD_DIM
QKV_WIDTH = Q_WIDTH + 2 * KV_WIDTH

VMEM_LIMIT_BYTES = 56 * 1024 * 1024
SUBLANES = 8
LANES = 128

ROWS_LARGE = 1824
ROWS_MEDIUM = 912
ROWS_SMALL = 304
FINAL_ROWS = 256
ATTN_Q_ROWS = 256
FFN_COLS = 256
PROJ_COLS = 512

F32 = jnp.float32
BF16 = jnp.bfloat16


def _params(semantics):
    return pltpu.CompilerParams(dimension_semantics=semantics,
                                vmem_limit_bytes=VMEM_LIMIT_BYTES)


def _rmsnorm_kernel(h_ref, g_ref, o_ref):
    x = h_ref[...]
    ms = jnp.mean(x * x, axis=-1, keepdims=True)
    o_ref[...] = (x * lax.rsqrt(ms + NORM_EPS) * g_ref[...]).astype(o_ref.dtype)


def _rmsnorm(h, gain, *, rows, out_dtype, block_rows, name):
    return pl.pallas_call(
        _rmsnorm_kernel,
        grid=(rows // block_rows,),
        in_specs=[pl.BlockSpec((block_rows, D_MODEL), lambda i: (i, 0)),
                  pl.BlockSpec((1, D_MODEL), lambda i: (0, 0))],
        out_specs=pl.BlockSpec((block_rows, D_MODEL), lambda i: (i, 0)),
        out_shape=jax.ShapeDtypeStruct((rows, D_MODEL), out_dtype),
        compiler_params=_params(("parallel",)),
        name=name,
    )(h, gain.reshape(1, D_MODEL))


def _mm_kernel(x_ref, w_ref, o_ref):
    o_ref[...] = jnp.dot(x_ref[...], w_ref[...].astype(BF16),
                         preferred_element_type=F32).astype(o_ref.dtype)


def _mm_res_kernel(x_ref, w_ref, h_ref, o_ref, *, scale):
    y = jnp.dot(x_ref[...], w_ref[...].astype(BF16), preferred_element_type=F32)
    o_ref[...] = h_ref[...] + scale * y


def _matmul(x, w, *, tm, tn, out_dtype, name, residual=None, scale=1.0):
    m, k = x.shape
    n = w.shape[1]
    x_spec = pl.BlockSpec((tm, k), lambda i, j: (i, 0),
                          pipeline_mode=pl.Buffered(1))
    w_spec = pl.BlockSpec((k, tn), lambda i, j: (0, j))
    o_spec = pl.BlockSpec((tm, tn), lambda i, j: (i, j))
    if residual is None:
        kernel, in_specs, args = _mm_kernel, [x_spec, w_spec], (x, w)
    else:
        kernel = functools.partial(_mm_res_kernel, scale=scale)
        in_specs, args = [x_spec, w_spec, o_spec], (x, w, residual)
    return pl.pallas_call(
        kernel,
        grid=(m // tm, n // tn),
        in_specs=in_specs,
        out_specs=o_spec,
        out_shape=jax.ShapeDtypeStruct((m, n), out_dtype),
        compiler_params=_params(("parallel", "arbitrary")),
        name=name,
    )(*args)


def _gate_up_kernel(x_ref, wg_ref, wu_ref, o_ref):
    x = x_ref[...]
    g = jnp.dot(x, wg_ref[...].astype(BF16), preferred_element_type=F32)
    u = jnp.dot(x, wu_ref[...].astype(BF16), preferred_element_type=F32)
    o_ref[...] = (g * jax.nn.sigmoid(g) * u).astype(o_ref.dtype)


def _gate_up(x, wg, wu, *, tm, tn, name):
    m, k = x.shape
    n = wg.shape[1]
    w_spec = pl.BlockSpec((k, tn), lambda i, j: (0, j))
    return pl.pallas_call(
        _gate_up_kernel,
        grid=(m // tm, n // tn),
        in_specs=[pl.BlockSpec((tm, k), lambda i, j: (i, 0),
                               pipeline_mode=pl.Buffered(1)),
                  w_spec, w_spec],
        out_specs=pl.BlockSpec((tm, tn), lambda i, j: (i, j)),
        out_shape=jax.ShapeDtypeStruct((m, n), BF16),
        compiler_params=_params(("parallel", "arbitrary")),
        name=name,
    )(x, wg, wu)


def _ffn(h, gain, wg, wu, wd, *, name):
    xn = _rmsnorm(h, gain, rows=M_ROWS, out_dtype=BF16, block_rows=ROWS_SMALL,
                  name=name + "_norm")
    act = _gate_up(xn, wg, wu, tm=ROWS_LARGE, tn=FFN_COLS, name=name + "_gate_up")
    return _matmul(act, wd, tm=ROWS_MEDIUM, tn=FFN_COLS, out_dtype=F32, residual=h,
                   scale=MACARON_WEIGHT, name=name + "_down")


CONV_ROWS = 512
CONV_COLS = 512
CONV_COL_BLOCKS = D_MODEL // CONV_COLS
CONV_TILES_PER_BATCH = SEQ // CONV_ROWS


def _conv_taps(u, u_left, u_right, b_gate, w_ref, bias_ref, o_ref):
    conv = (u_left * w_ref[0:1, :] + u * w_ref[1:2, :] + u_right * w_ref[2:3, :]
            + bias_ref[...])
    o_ref[...] = (b_gate * conv).astype(o_ref.dtype)


def _conv_real_kernel(b_ref, c_ref, h_ref, cp_ref, hp_ref, cn_ref, hn_ref,
                      w_ref, bias_ref, o_ref):
    rows = c_ref.shape[0]
    u = c_ref[...] * h_ref[...]
    u_prev = cp_ref[SUBLANES - 1:SUBLANES, :] * hp_ref[SUBLANES - 1:SUBLANES, :]
    last_in_batch = (pl.program_id(0) % CONV_TILES_PER_BATCH
                     == CONV_TILES_PER_BATCH - 1)
    u_next = jnp.where(last_in_batch, 0.0, cn_ref[0:1, :] * hn_ref[0:1, :])
    row = lax.broadcasted_iota(jnp.int32, u.shape, 0)
    u_left = jnp.where(row == 0, u_prev, pltpu.roll(u, 1, 0))
    u_right = jnp.where(row == rows - 1, u_next, pltpu.roll(u, rows - 1, 0))
    _conv_taps(u, u_left, u_right, b_ref[...], w_ref, bias_ref, o_ref)


def _conv_meta_kernel(b_ref, c_ref, h_ref, cn_ref, hn_ref, w_ref, bias_ref,
                      y_hbm_ref, o_ref):
    del y_hbm_ref
    u = c_ref[...] * h_ref[...]
    u_next = cn_ref[0:1, :] * hn_ref[0:1, :]
    row = lax.broadcasted_iota(jnp.int32, u.shape, 0)
    u_left = jnp.where(row == 0, 0.0, pltpu.roll(u, 1, 0))
    u_right = jnp.where(row == N_META - 1, u_next, pltpu.roll(u, N_META - 1, 0))
    _conv_taps(u, u_left, u_right, b_ref[...], w_ref, bias_ref, o_ref)


def _conv_gate(z, conv_w, conv_b):
    nb = CONV_COL_BLOCKS
    halo_per_tile = CONV_ROWS // SUBLANES
    meta_halo0 = (M_REAL + N_META - SUBLANES) // SUBLANES
    meta_halo_step = N_META // SUBLANES

    def prev_block(i):
        first = i % CONV_TILES_PER_BATCH == 0
        batch = i // CONV_TILES_PER_BATCH
        return jnp.where(first, meta_halo0 + meta_halo_step * batch,
                         i * halo_per_tile - 1)

    def next_block(i):
        last = i % CONV_TILES_PER_BATCH == CONV_TILES_PER_BATCH - 1
        return jnp.where(last, i * halo_per_tile, (i + 1) * halo_per_tile)

    def tile(col0):
        return pl.BlockSpec((CONV_ROWS, CONV_COLS), lambda i, j: (i, j + col0))

    def halo(block_fn, col0):
        return pl.BlockSpec((SUBLANES, CONV_COLS),
                            lambda i, j: (block_fn(i), j + col0))

    w_spec = pl.BlockSpec((3, CONV_COLS), lambda i, j: (0, j))
    bias_spec = pl.BlockSpec((1, CONV_COLS), lambda i, j: (0, j))
    bias = conv_b.reshape(1, D_MODEL)
    y = pl.pallas_call(
        _conv_real_kernel,
        grid=(M_REAL // CONV_ROWS, nb),
        in_specs=[tile(0), tile(nb), tile(2 * nb),
                  halo(prev_block, nb), halo(prev_block, 2 * nb),
                  halo(next_block, nb), halo(next_block, 2 * nb),
                  w_spec, bias_spec],
        out_specs=pl.BlockSpec((CONV_ROWS, CONV_COLS), lambda i, j: (i, j)),
        out_shape=jax.ShapeDtypeStruct((M_ROWS, D_MODEL), BF16),
        compiler_params=_params(("parallel", "parallel")),
        name="conv_real",
    )(z, z, z, z, z, z, z, conv_w, bias)

    meta0 = M_REAL // N_META
    real0_step = SEQ // SUBLANES

    def meta_tile(col0):
        return pl.BlockSpec((N_META, CONV_COLS), lambda b, j: (meta0 + b, j + col0))

    def first_real(col0):
        return pl.BlockSpec((SUBLANES, CONV_COLS),
                            lambda b, j: (b * real0_step, j + col0))

    w_spec = pl.BlockSpec((3, CONV_COLS), lambda b, j: (0, j))
    bias_spec = pl.BlockSpec((1, CONV_COLS), lambda b, j: (0, j))
    return pl.pallas_call(
        _conv_meta_kernel,
        grid=(BATCH, nb),
        in_specs=[meta_tile(0), meta_tile(nb), meta_tile(2 * nb),
                  first_real(nb), first_real(2 * nb), w_spec, bias_spec,
                  pl.BlockSpec(memory_space=pl.ANY)],
        out_specs=pl.BlockSpec((N_META, CONV_COLS), lambda b, j: (meta0 + b, j)),
        out_shape=jax.ShapeDtypeStruct((M_ROWS, D_MODEL), BF16),
        input_output_aliases={7: 0},
        compiler_params=_params(("parallel", "parallel")),
        name="conv_meta",
    )(z, z, z, z, z, conv_w, bias, y)


def _conv_mixer(h, gain, w_in, conv_w, conv_b, w_out):
    xn = _rmsnorm(h, gain, rows=M_ROWS, out_dtype=BF16, block_rows=ROWS_SMALL,
                  name="conv_norm")
    z = _matmul(xn, w_in, tm=ROWS_LARGE, tn=PROJ_COLS, out_dtype=F32, name="conv_in")
    y = _conv_gate(z, conv_w, conv_b)
    return _matmul(y, w_out, tm=ROWS_LARGE, tn=PROJ_COLS, out_dtype=F32, residual=h,
                   name="conv_out")


def _rope_tables():
    t = jnp.arange(SEQ, dtype=jnp.int32)
    real_row = (t // GRID_W).astype(F32)
    real_col = (t % GRID_W).astype(F32)
    meta_row = jnp.full((N_META,), -1.0, F32)
    meta_col = jnp.arange(N_META, dtype=F32)
    row = jnp.concatenate([jnp.tile(real_row, BATCH), jnp.tile(meta_row, BATCH)])
    col = jnp.concatenate([jnp.tile(real_col, BATCH), jnp.tile(meta_col, BATCH)])
    inv_freq = ROPE_THETA ** (-jnp.arange(0, ROPE_AXIS_DIM, 2, dtype=F32)
                              / ROPE_AXIS_DIM)
    ar = row[:, None] * inv_freq
    ac = col[:, None] * inv_freq
    zero = jnp.zeros_like(ar)
    cos = jnp.concatenate([jnp.cos(ar), jnp.cos(ar), jnp.cos(ac), jnp.cos(ac)], -1)
    sin_lo = jnp.concatenate([-jnp.sin(ar), zero, -jnp.sin(ac), zero], -1)
    sin_hi = jnp.concatenate([zero, jnp.sin(ar), zero, jnp.sin(ac)], -1)
    return cos, sin_lo, sin_hi


def _qk_prep_kernel(qkv_ref, cos_ref, slo_ref, shi_ref, qg_ref, kg_ref,
                    q_ref, k_ref, v_ref):
    cos = cos_ref[...]
    slo = slo_ref[...]
    shi = shi_ref[...]
    half = ROPE_AXIS_DIM // 2

    def prep(x, gain):
        ms = jnp.mean(x * x, axis=-1, keepdims=True)
        y = x * lax.rsqrt(ms + NORM_EPS) * gain
        return (y * cos + pltpu.roll(y, HEAD_DIM - half, 1) * slo
                + pltpu.roll(y, half, 1) * shi)

    q_gain = qg_ref[...] * (HEAD_DIM ** -0.5 * LOG2_E)
    k_gain = kg_ref[...]
    for hd in range(N_HEADS):
        sl = slice(hd * HEAD_DIM, (hd + 1) * HEAD_DIM)
        q_ref[:, sl] = prep(qkv_ref[:, sl], q_gain).astype(q_ref.dtype)
    for hd in range(N_KV_HEADS):
        sl = slice(hd * HEAD_DIM, (hd + 1) * HEAD_DIM)
        src = slice(Q_WIDTH + hd * HEAD_DIM, Q_WIDTH + (hd + 1) * HEAD_DIM)
        k_ref[:, sl] = prep(qkv_ref[:, src], k_gain).astype(k_ref.dtype)
    v_ref[...] = qkv_ref[:, Q_WIDTH + KV_WIDTH:].astype(v_ref.dtype)


def _qk_prep(qkv, q_gain, k_gain, *, block_rows):
    cos, slo, shi = _rope_tables()
    rows = lambda w: pl.BlockSpec((block_rows, w), lambda i: (i, 0))
    gain = pl.BlockSpec((1, HEAD_DIM), lambda i: (0, 0))
    return pl.pallas_call(
        _qk_prep_kernel,
        grid=(M_ROWS // block_rows,),
        in_specs=[rows(QKV_WIDTH), rows(HEAD_DIM), rows(HEAD_DIM), rows(HEAD_DIM),
                  gain, gain],
        out_specs=[rows(Q_WIDTH), rows(KV_WIDTH), rows(KV_WIDTH)],
        out_shape=[jax.ShapeDtypeStruct((M_ROWS, Q_WIDTH), BF16),
                   jax.ShapeDtypeStruct((M_ROWS, KV_WIDTH), BF16),
                   jax.ShapeDtypeStruct((M_ROWS, KV_WIDTH), BF16)],
        compiler_params=_params(("parallel",)),
        name="qk_prep",
    )(qkv, cos, slo, shi, q_gain.reshape(1, HEAD_DIM), k_gain.reshape(1, HEAD_DIM))


ATTN_KV_CHUNK = 512
ATTN_STRIP_ROWS = 32
META_PAD = 128

_NT = (((1,), (1,)), ((), ()))


def _flash_kernel(q_ref, k_ref, v_ref, km_ref, vm_ref, *rest, tq):
    o_ref, sa_ref, sb_ref, p_ref, m_ref, l_ref, acc_ref = rest[-7:]
    strip = min(ATTN_STRIP_ROWS, tq)
    n_chunks = SEQ // ATTN_KV_CHUNK

    m_ref[...] = jnp.full(m_ref.shape, -jnp.inf, F32)
    l_ref[...] = jnp.zeros(l_ref.shape, F32)
    acc_ref[...] = jnp.zeros(acc_ref.shape, F32)

    def head_rows(g):
        return slice(g * tq, (g + 1) * tq)

    def scores(g, kc, s_ref, width):
        q = q_ref[:, g * HEAD_DIM:(g + 1) * HEAD_DIM]
        s_ref[head_rows(g), :width] = lax.dot_general(
            q, kc, _NT, preferred_element_type=F32)

    def fold(g, s_ref, vc, width, n_valid):
        for r in range(tq // strip):
            rs = slice(g * tq + r * strip, g * tq + (r + 1) * strip)
            s = s_ref[rs, :width]
            if n_valid < width:
                col = lax.broadcasted_iota(jnp.int32, s.shape, 1)
                s = jnp.where(col < n_valid, s, -jnp.inf)
            m_old = m_ref[rs, :]
            m_new = jnp.maximum(m_old, jnp.max(s, axis=-1, keepdims=True))
            alpha = jnp.exp2(m_old - m_new)
            l_new = alpha * l_ref[rs, :]
            for j in range(width // LANES):
                cols = slice(j * LANES, (j + 1) * LANES)
                p = jnp.exp2(s[:, cols] - m_new)
                l_new = l_new + p
                p_ref[rs, cols] = p.astype(BF16)
            l_ref[rs, :] = l_new
            m_ref[rs, :] = m_new
            acc_ref[rs, :] = alpha * acc_ref[rs, :]
        acc_ref[head_rows(g), :] += jnp.dot(p_ref[head_rows(g), :width], vc,
                                            preferred_element_type=F32)

    def k_chunk(c):
        return k_ref[pl.ds(pl.multiple_of(c * ATTN_KV_CHUNK, ATTN_KV_CHUNK),
                           ATTN_KV_CHUNK), :]

    def v_chunk(c):
        return v_ref[pl.ds(pl.multiple_of(c * ATTN_KV_CHUNK, ATTN_KV_CHUNK),
                           ATTN_KV_CHUNK), :]

    pad = jnp.zeros((META_PAD - N_META, HEAD_DIM), BF16)
    k_meta = jnp.concatenate([km_ref[...], pad], axis=0)
    v_meta = jnp.concatenate([vm_ref[...], pad], axis=0)
    for g in range(GROUP):
        scores(g, k_meta, sb_ref, META_PAD)
        scores(g, k_chunk(0), sa_ref, ATTN_KV_CHUNK)
    for g in range(GROUP):
        fold(g, sb_ref, v_meta, META_PAD, N_META)

    def body(i, carry):
        c = 2 * i
        c_after = jnp.minimum(c + 2, n_chunks - 1)
        for g in range(GROUP):
            scores(g, k_chunk(c + 1), sb_ref, ATTN_KV_CHUNK)
            fold(g, sa_ref, v_chunk(c), ATTN_KV_CHUNK, ATTN_KV_CHUNK)
        for g in range(GROUP):
            scores(g, k_chunk(c_after), sa_ref, ATTN_KV_CHUNK)
            fold(g, sb_ref, v_chunk(c + 1), ATTN_KV_CHUNK, ATTN_KV_CHUNK)
        return carry

    lax.fori_loop(0, n_chunks // 2, body, 0)
    for g in range(GROUP):
        denom = jnp.sum(l_ref[head_rows(g), :], axis=-1, keepdims=True)
        o_ref[:, g * HEAD_DIM:(g + 1) * HEAD_DIM] = (
            acc_ref[head_rows(g), :] / denom).astype(o_ref.dtype)


def _flash_scratch(tq):
    nq = GROUP * tq
    return [pltpu.VMEM((nq, ATTN_KV_CHUNK), F32),
            pltpu.VMEM((nq, ATTN_KV_CHUNK), F32),
            pltpu.VMEM((nq, ATTN_KV_CHUNK), BF16),
            pltpu.VMEM((nq, LANES), F32),
            pltpu.VMEM((nq, LANES), F32),
            pltpu.VMEM((nq, HEAD_DIM), F32)]


def _attention(q, k, v, *, tq):
    qw = GROUP * HEAD_DIM
    n_q = SEQ // tq
    meta0 = M_REAL // N_META
    kv_real = pl.BlockSpec((SEQ, HEAD_DIM), lambda b, g, i: (b, g))
    kv_meta = pl.BlockSpec((N_META, HEAD_DIM), lambda b, g, i: (meta0 + b, g))
    q_spec = pl.BlockSpec((tq, qw), lambda b, g, i: (b * n_q + i, g))
    o_real = pl.pallas_call(
        functools.partial(_flash_kernel, tq=tq),
        grid=(BATCH, N_KV_HEADS, n_q),
        in_specs=[q_spec, kv_real, kv_real, kv_meta, kv_meta],
        out_specs=q_spec,
        out_shape=jax.ShapeDtypeStruct((M_ROWS, Q_WIDTH), BF16),
        scratch_shapes=_flash_scratch(tq),
        compiler_params=_params(("parallel", "parallel", "arbitrary")),
        name="attn_real",
    )(q, k, v, k, v)

    qm_spec = pl.BlockSpec((N_META, qw), lambda b, g, i: (meta0 + b, g))
    return pl.pallas_call(
        functools.partial(_flash_kernel, tq=N_META),
        grid=(BATCH, N_KV_HEADS, 1),
        in_specs=[qm_spec, kv_real, kv_real, kv_meta, kv_meta,
                  pl.BlockSpec(memory_space=pl.ANY)],
        out_specs=qm_spec,
        out_shape=jax.ShapeDtypeStruct((M_ROWS, Q_WIDTH), BF16),
        input_output_aliases={5: 0},
        scratch_shapes=_flash_scratch(N_META),
        compiler_params=_params(("parallel", "parallel", "arbitrary")),
        name="attn_meta",
    )(q, k, v, k, v, o_real)


def _attn_mixer(h, gain, w_qkv, q_gain, k_gain, w_o):
    xn = _rmsnorm(h, gain, rows=M_ROWS, out_dtype=BF16, block_rows=ROWS_SMALL,
                  name="attn_norm")
    qkv = _matmul(xn, w_qkv, tm=ROWS_LARGE, tn=PROJ_COLS, out_dtype=F32, name="attn_qkv")
    q, k, v = _qk_prep(qkv, q_gain, k_gain, block_rows=ROWS_SMALL)
    o = _attention(q, k, v, tq=ATTN_Q_ROWS)
    return _matmul(o, w_o, tm=ROWS_LARGE, tn=PROJ_COLS, out_dtype=F32, residual=h,
                   name="attn_out")


def kernel(x, meta_tokens, ffn_a_norm, ffn_a_w_gate, ffn_a_w_up, ffn_a_w_down, ffn_b_norm, ffn_b_w_gate, ffn_b_w_up, ffn_b_w_down, conv_norm, conv_w_in, conv_w, conv_b, conv_w_out, attn_norm, attn_w_qkv, attn_q_norm, attn_k_norm, attn_w_o, final_norm):
    bsz, n_real, d = x.shape
    assert (bsz, n_real, d) == (BATCH, SEQ, D_MODEL)
    meta = meta_tokens.astype(x.dtype)
    h = jnp.concatenate([x.reshape(M_REAL, D_MODEL)] + [meta] * BATCH, axis=0)
    bf = lambda w: w

    for i in range(DEPTH):
        h = _ffn(h, ffn_a_norm[i], bf(ffn_a_w_gate[i]), bf(ffn_a_w_up[i]),
                 bf(ffn_a_w_down[i]), name=f"ffn_a{i}")
        j = i // 2
        if i % 2 == 0:
            h = _conv_mixer(h, conv_norm[j], bf(conv_w_in[j]), conv_w[j],
                            conv_b[j], bf(conv_w_out[j]))
        else:
            h = _attn_mixer(h, attn_norm[j], bf(attn_w_qkv[j]), attn_q_norm[j],
                            attn_k_norm[j], bf(attn_w_o[j]))
        h = _ffn(h, ffn_b_norm[i], bf(ffn_b_w_gate[i]), bf(ffn_b_w_up[i]),
                 bf(ffn_b_w_down[i]), name=f"ffn_b{i}")

    out = _rmsnorm(h, final_norm, rows=M_REAL, out_dtype=F32, block_rows=FINAL_ROWS,
                   name="final_norm")
    return out.reshape(BATCH, SEQ, D_MODEL)
```

```python
import functools

import jax
import jax.numpy as jnp
from jax import lax
from jax.experimental import pallas as pl
from jax.experimental.pallas import tpu as pltpu

D_MODEL = 4096
BATCH = 2
SEQ = 8192
DEPTH = 2
N_META = 16
GRID_W = 64
D_FF = 11008
N_HEADS = 32
N_KV_HEADS = 8
HEAD_DIM = 128
GROUP = N_HEADS // N_KV_HEADS
ROPE_AXIS_DIM = HEAD_DIM // 2
ROPE_THETA = 10000.0
NORM_EPS = 1e-6
LOG2_E = 1.4426950408889634
MACARON_WEIGHT = 0.5

M_REAL = BATCH * SEQ
M_META = BATCH * N_META
M_ROWS = M_REAL + M_META
Q_WIDTH = N_HEADS * HEAD_DIM
KV_WIDTH = N_KV_HEADS * HEAD_DIM
QKV_WIDTH = Q_WIDTH + 2 * KV_WIDTH

VMEM_LIMIT_BYTES = 56 * 1024 * 1024
SUBLANES = 8
LANES = 128

ROWS_LARGE = 1824
ROWS_MEDIUM = 912
ROWS_SMALL = 304
FINAL_ROWS = 256
ATTN_Q_ROWS = 256
FFN_COLS = 256
PROJ_COLS = 512

F32 = jnp.float32
BF16 = jnp.bfloat16


def _params(semantics):
    return pltpu.CompilerParams(dimension_semantics=semantics,
                                vmem_limit_bytes=VMEM_LIMIT_BYTES)


def _lane_partial_sumsq(y):
    sq = y * y
    part = sq[:, :LANES]
    for c in range(1, y.shape[1] // LANES):
        part = part + sq[:, c * LANES:(c + 1) * LANES]
    return part


def _row_scale(ss_ref):
    total = jnp.sum(ss_ref[...], axis=-1, keepdims=True)
    return lax.rsqrt(total * (1.0 / D_MODEL) + NORM_EPS)


def _prenorm_kernel(h_ref, g_ref, hg_ref, ss_ref):
    x = h_ref[...]
    hg_ref[...] = (x * g_ref[...]).astype(hg_ref.dtype)
    ss_ref[...] = _lane_partial_sumsq(x)


def _prenorm(h, gain, *, block_rows, name):
    rows = h.shape[0]
    row_spec = lambda w: pl.BlockSpec((block_rows, w), lambda i: (i, 0))
    return pl.pallas_call(
        _prenorm_kernel,
        grid=(rows // block_rows,),
        in_specs=[row_spec(D_MODEL), pl.BlockSpec((1, D_MODEL), lambda i: (0, 0))],
        out_specs=[row_spec(D_MODEL), row_spec(LANES)],
        out_shape=[jax.ShapeDtypeStruct((rows, D_MODEL), BF16),
                   jax.ShapeDtypeStruct((rows, LANES), F32)],
        compiler_params=_params(("parallel",)),
        name=name,
    )(h, gain.reshape(1, D_MODEL))


def _rmsnorm_kernel(h_ref, g_ref, o_ref):
    x = h_ref[...]
    ms = jnp.mean(x * x, axis=-1, keepdims=True)
    o_ref[...] = (x * lax.rsqrt(ms + NORM_EPS) * g_ref[...]).astype(o_ref.dtype)


def _rmsnorm(h, gain, *, rows, out_dtype, block_rows, name):
    return pl.pallas_call(
        _rmsnorm_kernel,
        grid=(rows // block_rows,),
        in_specs=[pl.BlockSpec((block_rows, D_MODEL), lambda i: (i, 0)),
                  pl.BlockSpec((1, D_MODEL), lambda i: (0, 0))],
        out_specs=pl.BlockSpec((block_rows, D_MODEL), lambda i: (i, 0)),
        out_shape=jax.ShapeDtypeStruct((rows, D_MODEL), out_dtype),
        compiler_params=_params(("parallel",)),
        name=name,
    )(h, gain.reshape(1, D_MODEL))


def _panel_spec(tm, k):
    return pl.BlockSpec((tm, k), lambda i, j: (i, 0), pipeline_mode=pl.Buffered(1))


def _weight_spec(layer, k, tn):
    return pl.BlockSpec((None, k, tn), lambda i, j: (layer, 0, j))


def _ss_spec(tm):
    return pl.BlockSpec((tm, LANES), lambda i, j: (i, 0))


def _normed_mm_kernel(x_ref, ss_ref, w_ref, o_ref):
    y = jnp.dot(x_ref[...], w_ref[...].astype(BF16), preferred_element_type=F32)
    o_ref[...] = (_row_scale(ss_ref) * y).astype(o_ref.dtype)


def _normed_matmul(stream, w, layer, *, tm, tn, out_dtype, name):
    hg, ss = stream
    m, k = hg.shape
    n = w.shape[2]
    return pl.pallas_call(
        _normed_mm_kernel,
        grid=(m // tm, n // tn),
        in_specs=[_panel_spec(tm, k), _ss_spec(tm), _weight_spec(layer, k, tn)],
        out_specs=pl.BlockSpec((tm, tn), lambda i, j: (i, j)),
        out_shape=jax.ShapeDtypeStruct((m, n), out_dtype),
        compiler_params=_params(("parallel", "arbitrary")),
        name=name,
    )(hg, ss, w)


def _gate_up_kernel(x_ref, ss_ref, wg_ref, wu_ref, o_ref):
    x = x_ref[...]
    r = _row_scale(ss_ref)
    g = r * jnp.dot(x, wg_ref[...].astype(BF16), preferred_element_type=F32)
    u = r * jnp.dot(x, wu_ref[...].astype(BF16), preferred_element_type=F32)
    o_ref[...] = (g * jax.nn.sigmoid(g) * u).astype(o_ref.dtype)


def _gate_up(stream, wg, wu, layer, *, tm, tn, name):
    hg, ss = stream
    m, k = hg.shape
    n = wg.shape[2]
    return pl.pallas_call(
        _gate_up_kernel,
        grid=(m // tm, n // tn),
        in_specs=[_panel_spec(tm, k), _ss_spec(tm),
                  _weight_spec(layer, k, tn), _weight_spec(layer, k, tn)],
        out_specs=pl.BlockSpec((tm, tn), lambda i, j: (i, j)),
        out_shape=jax.ShapeDtypeStruct((m, n), BF16),
        compiler_params=_params(("parallel", "arbitrary")),
        name=name,
    )(hg, ss, wg, wu)


def _residual_kernel(x_ref, w_ref, h_ref, o_ref, *, scale):
    y = jnp.dot(x_ref[...], w_ref[...].astype(BF16), preferred_element_type=F32)
    o_ref[...] = h_ref[...] + scale * y


def _residual_prenorm_kernel(x_ref, w_ref, h_ref, g_ref, o_ref, hg_ref, ss_ref,
                             *, scale):
    y = jnp.dot(x_ref[...], w_ref[...].astype(BF16), preferred_element_type=F32)
    y = h_ref[...] + scale * y
    o_ref[...] = y
    hg_ref[...] = (y * g_ref[...]).astype(hg_ref.dtype)
    part = _lane_partial_sumsq(y)
    j = pl.program_id(1)

    @pl.when(j == 0)
    def _():
        ss_ref[...] = part

    @pl.when(j > 0)
    def _():
        ss_ref[...] += part


def _residual_matmul(x, w, layer, h, *, tm, tn, scale, name, next_gain=None):
    m, k = x.shape
    n = w.shape[2]
    tile = pl.BlockSpec((tm, tn), lambda i, j: (i, j))
    in_specs = [_panel_spec(tm, k), _weight_spec(layer, k, tn), tile]
    h_shape = jax.ShapeDtypeStruct((m, n), F32)
    if next_gain is None:
        return pl.pallas_call(
            functools.partial(_residual_kernel, scale=scale),
            grid=(m // tm, n // tn),
            in_specs=in_specs,
            out_specs=tile,
            out_shape=h_shape,
            compiler_params=_params(("parallel", "arbitrary")),
            name=name,
        )(x, w, h), None
    h_new, hg, ss = pl.pallas_call(
        functools.partial(_residual_prenorm_kernel, scale=scale),
        grid=(m // tm, n // tn),
        in_specs=in_specs + [pl.BlockSpec((1, tn), lambda i, j: (0, j))],
        out_specs=[tile, tile, _ss_spec(tm)],
        out_shape=[h_shape, jax.ShapeDtypeStruct((m, n), BF16),
                   jax.ShapeDtypeStruct((m, LANES), F32)],
        compiler_params=_params(("parallel", "arbitrary")),
        name=name,
    )(x, w, h, next_gain.reshape(1, n))
    return h_new, (hg, ss)


def _ffn(h, stream, wg, wu, wd, layer, next_gain, *, name):
    act = _gate_up(stream, wg, wu, layer, tm=ROWS_LARGE, tn=FFN_COLS,
                   name=name + "_gate_up")
    return _residual_matmul(act, wd, layer, h, tm=ROWS_MEDIUM, tn=FFN_COLS,
                            scale=MACARON_WEIGHT, next_gain=next_gain,
                            name=name + "_down")


CONV_ROWS = 512
CONV_COLS = 512
CONV_COL_BLOCKS = D_MODEL // CONV_COLS
CONV_TILES_PER_BATCH = SEQ // CONV_ROWS


def _conv_taps(u, u_left, u_right, b_gate, w_ref, bias_ref, o_ref):
    conv = (u_left * w_ref[0:1, :] + u * w_ref[1:2, :] + u_right * w_ref[2:3, :]
            + bias_ref[...])
    o_ref[...] = (b_gate * conv).astype(o_ref.dtype)


def _conv_real_kernel(b_ref, c_ref, h_ref, cp_ref, hp_ref, cn_ref, hn_ref,
                      w_ref, bias_ref, o_ref):
    rows = c_ref.shape[0]
    u = c_ref[...] * h_ref[...]
    u_prev = cp_ref[SUBLANES - 1:SUBLANES, :] * hp_ref[SUBLANES - 1:SUBLANES, :]
    last_in_batch = (pl.program_id(0) % CONV_TILES_PER_BATCH
                     == CONV_TILES_PER_BATCH - 1)
    u_next = jnp.where(last_in_batch, 0.0, cn_ref[0:1, :] * hn_ref[0:1, :])
    row = lax.broadcasted_iota(jnp.int32, u.shape, 0)
    u_left = jnp.where(row == 0, u_prev, pltpu.roll(u, 1, 0))
    u_right = jnp.where(row == rows - 1, u_next, pltpu.roll(u, rows - 1, 0))
    _conv_taps(u, u_left, u_right, b_ref[...], w_ref, bias_ref, o_ref)


def _conv_meta_kernel(b_ref, c_ref, h_ref, cn_ref, hn_ref, w_ref, bias_ref,
                      y_hbm_ref, o_ref):
    del y_hbm_ref
    u = c_ref[...] * h_ref[...]
    u_next = cn_ref[0:1, :] * hn_ref[0:1, :]
    row = lax.broadcasted_iota(jnp.int32, u.shape, 0)
    u_left = jnp.where(row == 0, 0.0, pltpu.roll(u, 1, 0))
    u_right = jnp.where(row == N_META - 1, u_next, pltpu.roll(u, N_META - 1, 0))
    _conv_taps(u, u_left, u_right, b_ref[...], w_ref, bias_ref, o_ref)


def _conv_gate(z, conv_w, conv_b):
    nb = CONV_COL_BLOCKS
    halo_per_tile = CONV_ROWS // SUBLANES
    meta_halo0 = (M_REAL + N_META - SUBLANES) // SUBLANES
    meta_halo_step = N_META // SUBLANES

    def prev_block(i):
        first = i % CONV_TILES_PER_BATCH == 0
        batch = i // CONV_TILES_PER_BATCH
        return jnp.where(first, meta_halo0 + meta_halo_step * batch,
                         i * halo_per_tile - 1)

    def next_block(i):
        last = i % CONV_TILES_PER_BATCH == CONV_TILES_PER_BATCH - 1
        return jnp.where(last, i * halo_per_tile, (i + 1) * halo_per_tile)

    def tile(col0):
        return pl.BlockSpec((CONV_ROWS, CONV_COLS), lambda i, j: (i, j + col0))

    def halo(block_fn, col0):
        return pl.BlockSpec((SUBLANES, CONV_COLS),
                            lambda i, j: (block_fn(i), j + col0))

    w_spec = pl.BlockSpec((3, CONV_COLS), lambda i, j: (0, j))
    bias_spec = pl.BlockSpec((1, CONV_COLS), lambda i, j: (0, j))
    bias = conv_b.reshape(1, D_MODEL)
    y = pl.pallas_call(
        _conv_real_kernel,
        grid=(M_REAL // CONV_ROWS, nb),
        in_specs=[tile(0), tile(nb), tile(2 * nb),
                  halo(prev_block, nb), halo(prev_block, 2 * nb),
                  halo(next_block, nb), halo(next_block, 2 * nb),
                  w_spec, bias_spec],
        out_specs=pl.BlockSpec((CONV_ROWS, CONV_COLS), lambda i, j: (i, j)),
        out_shape=jax.ShapeDtypeStruct((M_ROWS, D_MODEL), BF16),
        compiler_params=_params(("parallel", "parallel")),
        name="conv_real",
    )(z, z, z, z, z, z, z, conv_w, bias)

    meta0 = M_REAL // N_META
    real0_step = SEQ // SUBLANES

    def meta_tile(col0):
        return pl.BlockSpec((N_META, CONV_COLS), lambda b, j: (meta0 + b, j + col0))

    def first_real(col0):
        return pl.BlockSpec((SUBLANES, CONV_COLS),
                            lambda b, j: (b * real0_step, j + col0))

    w_spec = pl.BlockSpec((3, CONV_COLS), lambda b, j: (0, j))
    bias_spec = pl.BlockSpec((1, CONV_COLS), lambda b, j: (0, j))
    return pl.pallas_call(
        _conv_meta_kernel,
        grid=(BATCH, nb),
        in_specs=[meta_tile(0), meta_tile(nb), meta_tile(2 * nb),
                  first_real(nb), first_real(2 * nb), w_spec, bias_spec,
                  pl.BlockSpec(memory_space=pl.ANY)],
        out_specs=pl.BlockSpec((N_META, CONV_COLS), lambda b, j: (meta0 + b, j)),
        out_shape=jax.ShapeDtypeStruct((M_ROWS, D_MODEL), BF16),
        input_output_aliases={7: 0},
        compiler_params=_params(("parallel", "parallel")),
        name="conv_meta",
    )(z, z, z, z, z, conv_w, bias, y)


def _conv_mixer(h, stream, w_in, conv_w, conv_b, w_out, layer, next_gain):
    z = _normed_matmul(stream, w_in, layer, tm=ROWS_LARGE, tn=PROJ_COLS,
                       out_dtype=F32, name="conv_in")
    y = _conv_gate(z, conv_w[layer], conv_b[layer])
    return _residual_matmul(y, w_out, layer, h, tm=ROWS_LARGE, tn=PROJ_COLS,
                            scale=1.0, next_gain=next_gain, name="conv_out")


def _rope_tables():
    t = jnp.arange(SEQ, dtype=jnp.int32)
    real_row = (t // GRID_W).astype(F32)
    real_col = (t % GRID_W).astype(F32)
    meta_row = jnp.full((N_META,), -1.0, F32)
    meta_col = jnp.arange(N_META, dtype=F32)
    row = jnp.concatenate([jnp.tile(real_row, BATCH), jnp.tile(meta_row, BATCH)])
    col = jnp.concatenate([jnp.tile(real_col, BATCH), jnp.tile(meta_col, BATCH)])
    inv_freq = ROPE_THETA ** (-jnp.arange(0, ROPE_AXIS_DIM, 2, dtype=F32)
                              / ROPE_AXIS_DIM)
    ar = row[:, None] * inv_freq
    ac = col[:, None] * inv_freq
    zero = jnp.zeros_like(ar)
    cos = jnp.concatenate([jnp.cos(ar), jnp.cos(ar), jnp.cos(ac), jnp.cos(ac)], -1)
    sin_lo = jnp.concatenate([-jnp.sin(ar), zero, -jnp.sin(ac), zero], -1)
    sin_hi = jnp.concatenate([zero, jnp.sin(ar), zero, jnp.sin(ac)], -1)
    return cos, sin_lo, sin_hi


def _qk_prep_kernel(qkv_ref, cos_ref, slo_ref, shi_ref, qg_ref, kg_ref,
                    q_ref, k_ref, v_ref):
    cos = cos_ref[...]
    slo = slo_ref[...]
    shi = shi_ref[...]
    half = ROPE_AXIS_DIM // 2

    def prep(x, gain):
        ms = jnp.mean(x * x, axis=-1, keepdims=True)
        y = x * lax.rsqrt(ms + NORM_EPS) * gain
        return (y * cos + pltpu.roll(y, HEAD_DIM - half, 1) * slo
                + pltpu.roll(y, half, 1) * shi)

    q_gain = qg_ref[...] * (HEAD_DIM ** -0.5 * LOG2_E)
    k_gain = kg_ref[...]
    for hd in range(N_HEADS):
        sl = slice(hd * HEAD_DIM, (hd + 1) * HEAD_DIM)
        q_ref[:, sl] = prep(qkv_ref[:, sl], q_gain).astype(q_ref.dtype)
    for hd in range(N_KV_HEADS):
        sl = slice(hd * HEAD_DIM, (hd + 1) * HEAD_DIM)
        src = slice(Q_WIDTH + hd * HEAD_DIM, Q_WIDTH + (hd + 1) * HEAD_DIM)
        k_ref[:, sl] = prep(qkv_ref[:, src], k_gain).astype(k_ref.dtype)
    v_ref[...] = qkv_ref[:, Q_WIDTH + KV_WIDTH:].astype(v_ref.dtype)


def _qk_prep(qkv, q_gain, k_gain, *, block_rows):
    cos, slo, shi = _rope_tables()
    rows = lambda w: pl.BlockSpec((block_rows, w), lambda i: (i, 0))
    gain = pl.BlockSpec((1, HEAD_DIM), lambda i: (0, 0))
    return pl.pallas_call(
        _qk_prep_kernel,
        grid=(M_ROWS // block_rows,),
        in_specs=[rows(QKV_WIDTH), rows(HEAD_DIM), rows(HEAD_DIM), rows(HEAD_DIM),
                  gain, gain],
        out_specs=[rows(Q_WIDTH), rows(KV_WIDTH), rows(KV_WIDTH)],
        out_shape=[jax.ShapeDtypeStruct((M_ROWS, Q_WIDTH), BF16),
                   jax.ShapeDtypeStruct((M_ROWS, KV_WIDTH), BF16),
                   jax.ShapeDtypeStruct((M_ROWS, KV_WIDTH), BF16)],
        compiler_params=_params(("parallel",)),
        name="qk_prep",
    )(qkv, cos, slo, shi, q_gain.reshape(1, HEAD_DIM), k_gain.reshape(1, HEAD_DIM))


ATTN_KV_CHUNK = 512
ATTN_STRIP_ROWS = 32
META_PAD = 128

_NT = (((1,), (1,)), ((), ()))


def _flash_kernel(q_ref, k_ref, v_ref, km_ref, vm_ref, *rest, tq):
    o_ref, sa_ref, sb_ref, p_ref, m_ref, l_ref, acc_ref = rest[-7:]
    strip = min(ATTN_STRIP_ROWS, tq)
    n_chunks = SEQ // ATTN_KV_CHUNK

    m_ref[...] = jnp.full(m_ref.shape, -jnp.inf, F32)
    l_ref[...] = jnp.zeros(l_ref.shape, F32)
    acc_ref[...] = jnp.zeros(acc_ref.shape, F32)

    def head_rows(g):
        return slice(g * tq, (g + 1) * tq)

    def scores(g, kc, s_ref, width):
        q = q_ref[:, g * HEAD_DIM:(g + 1) * HEAD_DIM]
        s_ref[head_rows(g), :width] = lax.dot_general(
            q, kc, _NT, preferred_element_type=F32)

    def fold(g, s_ref, vc, width, n_valid):
        for r in range(tq // strip):
            rs = slice(g * tq + r * strip, g * tq + (r + 1) * strip)
            s = s_ref[rs, :width]
            if n_valid < width:
                col = lax.broadcasted_iota(jnp.int32, s.shape, 1)
                s = jnp.where(col < n_valid, s, -jnp.inf)
            m_old = m_ref[rs, :]
            m_new = jnp.maximum(m_old, jnp.max(s, axis=-1, keepdims=True))
            alpha = jnp.exp2(m_old - m_new)
            l_new = alpha * l_ref[rs, :]
            for j in range(width // LANES):
                cols = slice(j * LANES, (j + 1) * LANES)
                p = jnp.exp2(s[:, cols] - m_new)
                l_new = l_new + p
                p_ref[rs, cols] = p.astype(BF16)
            l_ref[rs, :] = l_new
            m_ref[rs, :] = m_new
            acc_ref[rs, :] = alpha * acc_ref[rs, :]
        acc_ref[head_rows(g), :] += jnp.dot(p_ref[head_rows(g), :width], vc,
                                            preferred_element_type=F32)

    def k_chunk(c):
        return k_ref[pl.ds(pl.multiple_of(c * ATTN_KV_CHUNK, ATTN_KV_CHUNK),
                           ATTN_KV_CHUNK), :]

    def v_chunk(c):
        return v_ref[pl.ds(pl.multiple_of(c * ATTN_KV_CHUNK, ATTN_KV_CHUNK),
                           ATTN_KV_CHUNK), :]

    pad = jnp.zeros((META_PAD - N_META, HEAD_DIM), BF16)
    k_meta = jnp.concatenate([km_ref[...], pad], axis=0)
    v_meta = jnp.concatenate([vm_ref[...], pad], axis=0)
    for g in range(GROUP):
        scores(g, k_meta, sb_ref, META_PAD)
        scores(g, k_chunk(0), sa_ref, ATTN_KV_CHUNK)
    for g in range(GROUP):
        fold(g, sb_ref, v_meta, META_PAD, N_META)

    def body(i, carry):
        c = 2 * i
        c_after = jnp.minimum(c + 2, n_chunks - 1)
        for g in range(GROUP):
            scores(g, k_chunk(c + 1), sb_ref, ATTN_KV_CHUNK)
            fold(g, sa_ref, v_chunk(c), ATTN_KV_CHUNK, ATTN_KV_CHUNK)
        for g in range(GROUP):
            scores(g, k_chunk(c_after), sa_ref, ATTN_KV_CHUNK)
            fold(g, sb_ref, v_chunk(c + 1), ATTN_KV_CHUNK, ATTN_KV_CHUNK)
        return carry

    lax.fori_loop(0, n_chunks // 2, body, 0)
    for g in range(GROUP):
        denom = jnp.sum(l_ref[head_rows(g), :], axis=-1, keepdims=True)
        o_ref[:, g * HEAD_DIM:(g + 1) * HEAD_DIM] = (
            acc_ref[head_rows(g), :] / denom).astype(o_ref.dtype)


def _flash_scratch(tq):
    nq = GROUP * tq
    return [pltpu.VMEM((nq, ATTN_KV_CHUNK), F32),
            pltpu.VMEM((nq, ATTN_KV_CHUNK), F32),
            pltpu.VMEM((nq, ATTN_KV_CHUNK), BF16),
            pltpu.VMEM((nq, LANES), F32),
            pltpu.VMEM((nq, LANES), F32),
            pltpu.VMEM((nq, HEAD_DIM), F32)]


def _attention(q, k, v, *, tq):
    qw = GROUP * HEAD_DIM
    n_q = SEQ // tq
    meta0 = M_REAL // N_META
    kv_real = pl.BlockSpec((SEQ, HEAD_DIM), lambda b, g, i: (b, g))
    kv_meta = pl.BlockSpec((N_META, HEAD_DIM), lambda b, g, i: (meta0 + b, g))
    q_spec = pl.BlockSpec((tq, qw), lambda b, g, i: (b * n_q + i, g))
    o_real = pl.pallas_call(
        functools.partial(_flash_kernel, tq=tq),
        grid=(BATCH, N_KV_HEADS, n_q),
        in_specs=[q_spec, kv_real, kv_real, kv_meta, kv_meta],
        out_specs=q_spec,
        out_shape=jax.ShapeDtypeStruct((M_ROWS, Q_WIDTH), BF16),
        scratch_shapes=_flash_scratch(tq),
        compiler_params=_params(("parallel", "parallel", "arbitrary")),
        name="attn_real",
    )(q, k, v, k, v)

    qm_spec = pl.BlockSpec((N_META, qw), lambda b, g, i: (meta0 + b, g))
    return pl.pallas_call(
        functools.partial(_flash_kernel, tq=N_META),
        grid=(BATCH, N_KV_HEADS, 1),
        in_specs=[qm_spec, kv_real, kv_real, kv_meta, kv_meta,
                  pl.BlockSpec(memory_space=pl.ANY)],
        out_specs=qm_spec,
        out_shape=jax.ShapeDtypeStruct((M_ROWS, Q_WIDTH), BF16),
        input_output_aliases={5: 0},
        scratch_shapes=_flash_scratch(N_META),
        compiler_params=_params(("parallel", "parallel", "arbitrary")),
        name="attn_meta",
    )(q, k, v, k, v, o_real)


def _attn_mixer(h, stream, w_qkv, q_gain, k_gain, w_o, layer, next_gain):
    qkv = _normed_matmul(stream, w_qkv, layer, tm=ROWS_LARGE, tn=PROJ_COLS,
                         out_dtype=F32, name="attn_qkv")
    q, k, v = _qk_prep(qkv, q_gain[layer], k_gain[layer], block_rows=ROWS_SMALL)
    o = _attention(q, k, v, tq=ATTN_Q_ROWS)
    return _residual_matmul(o, w_o, layer, h, tm=ROWS_LARGE, tn=PROJ_COLS,
                            scale=1.0, next_gain=next_gain, name="attn_out")


def kernel(x, meta_tokens, ffn_a_norm, ffn_a_w_gate, ffn_a_w_up, ffn_a_w_down, ffn_b_norm, ffn_b_w_gate, ffn_b_w_up, ffn_b_w_down, conv_norm, conv_w_in, conv_w, conv_b, conv_w_out, attn_norm, attn_w_qkv, attn_q_norm, attn_k_norm, attn_w_o, final_norm):
    bsz, n_real, d = x.shape
    assert (bsz, n_real, d) == (BATCH, SEQ, D_MODEL)
    meta = meta_tokens.astype(x.dtype)
    h = jnp.concatenate([x.reshape(M_REAL, D_MODEL)] + [meta] * BATCH, axis=0)
    stream = _prenorm(h, ffn_a_norm[0], block_rows=ROWS_SMALL, name="embed_norm")

    for i in range(DEPTH):
        j = i // 2
        is_conv = i % 2 == 0
        mixer_gain = conv_norm[j] if is_conv else attn_norm[j]
        h, stream = _ffn(h, stream, ffn_a_w_gate, ffn_a_w_up, ffn_a_w_down, i,
                         mixer_gain, name=f"ffn_a{i}")
        if is_conv:
            h, stream = _conv_mixer(h, stream, conv_w_in, conv_w, conv_b,
                                    conv_w_out, j, ffn_b_norm[i])
        else:
            h, stream = _attn_mixer(h, stream, attn_w_qkv, attn_q_norm,
                                    attn_k_norm, attn_w_o, j, ffn_b_norm[i])
        after = ffn_a_norm[i + 1] if i + 1 < DEPTH else None
        h, stream = _ffn(h, stream, ffn_b_w_gate, ffn_b_w_up, ffn_b_w_down, i,
                         after, name=f"ffn_b{i}")

    out = _rmsnorm(h, final_norm, rows=M_REAL, out_dtype=F32, block_rows=FINAL_ROWS,
                   name="final_norm")
    return out.reshape(BATCH, SEQ, D_MODEL)
```

```python
import functools

import jax
import jax.numpy as jnp
from jax import lax
from jax.experimental import pallas as pl
from jax.experimental.pallas import tpu as pltpu

D_MODEL = 4096
BATCH = 2
SEQ = 8192
DEPTH = 2
N_META = 16
GRID_W = 64
D_FF = 11008
N_HEADS = 32
N_KV_HEADS = 8
HEAD_DIM = 128
GROUP = N_HEADS // N_KV_HEADS
ROPE_AXIS_DIM = HEAD_DIM // 2
ROPE_THETA = 10000.0
NORM_EPS = 1e-6
LOG2_E = 1.4426950408889634
MACARON_WEIGHT = 0.5

M_REAL = BATCH * SEQ
M_META = BATCH * N_META
M_ROWS = M_REAL + M_META
Q_WIDTH = N_HEADS * HEAD_DIM
KV_WIDTH = N_KV_HEADS * HEAD_DIM
QKV_WIDTH = Q_WIDTH + 2 * KV_WIDTH

VMEM_LIMIT_BYTES = 56 * 1024 * 1024
SUBLANES = 8
LANES = 128

ROWS_LARGE = 1824
ROWS_MEDIUM = 912
ROWS_SMALL = 304
FINAL_ROWS = 256
ATTN_Q_ROWS = 256
FFN_COLS = 256
PROJ_COLS = 512

F32 = jnp.float32
BF16 = jnp.bfloat16


def _params(semantics):
    return pltpu.CompilerParams(dimension_semantics=semantics,
                                vmem_limit_bytes=VMEM_LIMIT_BYTES)


def _lane_partial_sumsq(y):
    sq = y * y
    part = sq[:, :LANES]
    for c in range(1, y.shape[1] // LANES):
        part = part + sq[:, c * LANES:(c + 1) * LANES]
    return part


def _row_scale(ss_ref):
    total = jnp.sum(ss_ref[...], axis=-1, keepdims=True)
    return lax.rsqrt(total * (1.0 / D_MODEL) + NORM_EPS)


def _prenorm_kernel(h_ref, g_ref, hg_ref, ss_ref):
    x = h_ref[...]
    hg_ref[...] = (x * g_ref[...]).astype(hg_ref.dtype)
    ss_ref[...] = _lane_partial_sumsq(x)


def _prenorm(h, gain, *, block_rows, name):
    rows = h.shape[0]
    row_spec = lambda w: pl.BlockSpec((block_rows, w), lambda i: (i, 0))
    return pl.pallas_call(
        _prenorm_kernel,
        grid=(rows // block_rows,),
        in_specs=[row_spec(D_MODEL), pl.BlockSpec((1, D_MODEL), lambda i: (0, 0))],
        out_specs=[row_spec(D_MODEL), row_spec(LANES)],
        out_shape=[jax.ShapeDtypeStruct((rows, D_MODEL), BF16),
                   jax.ShapeDtypeStruct((rows, LANES), F32)],
        compiler_params=_params(("parallel",)),
        name=name,
    )(h, gain.reshape(1, D_MODEL))


def _rmsnorm_kernel(h_ref, g_ref, o_ref):
    x = h_ref[...]
    ms = jnp.mean(x * x, axis=-1, keepdims=True)
    o_ref[...] = (x * lax.rsqrt(ms + NORM_EPS) * g_ref[...]).astype(o_ref.dtype)


def _rmsnorm(h, gain, *, rows, out_dtype, block_rows, name):
    return pl.pallas_call(
        _rmsnorm_kernel,
        grid=(rows // block_rows,),
        in_specs=[pl.BlockSpec((block_rows, D_MODEL), lambda i: (i, 0)),
                  pl.BlockSpec((1, D_MODEL), lambda i: (0, 0))],
        out_specs=pl.BlockSpec((block_rows, D_MODEL), lambda i: (i, 0)),
        out_shape=jax.ShapeDtypeStruct((rows, D_MODEL), out_dtype),
        compiler_params=_params(("parallel",)),
        name=name,
    )(h, gain.reshape(1, D_MODEL))


def _panel_spec(tm, k):
    return pl.BlockSpec((tm, k), lambda i, j: (i, 0), pipeline_mode=pl.Buffered(1))


def _weight_spec(layer, k, tn):
    return pl.BlockSpec((None, k, tn), lambda i, j: (layer, 0, j))


def _ss_spec(tm):
    return pl.BlockSpec((tm, LANES), lambda i, j: (i, 0))


def _normed_mm_kernel(x_ref, ss_ref, w_ref, o_ref):
    y = jnp.dot(x_ref[...], w_ref[...].astype(BF16), preferred_element_type=F32)
    o_ref[...] = (_row_scale(ss_ref) * y).astype(o_ref.dtype)


def _normed_matmul(stream, w, layer, *, tm, tn, out_dtype, name):
    hg, ss = stream
    m, k = hg.shape
    n = w.shape[2]
    return pl.pallas_call(
        _normed_mm_kernel,
        grid=(m // tm, n // tn),
        in_specs=[_panel_spec(tm, k), _ss_spec(tm), _weight_spec(layer, k, tn)],
        out_specs=pl.BlockSpec((tm, tn), lambda i, j: (i, j)),
        out_shape=jax.ShapeDtypeStruct((m, n), out_dtype),
        compiler_params=_params(("parallel", "arbitrary")),
        name=name,
    )(hg, ss, w)


def _gate_up_kernel(x_ref, ss_ref, wg_ref, wu_ref, o_ref):
    x = x_ref[...]
    r = _row_scale(ss_ref)
    g = r * jnp.dot(x, wg_ref[...].astype(BF16), preferred_element_type=F32)
    u = r * jnp.dot(x, wu_ref[...].astype(BF16), preferred_element_type=F32)
    o_ref[...] = (g * jax.nn.sigmoid(g) * u).astype(o_ref.dtype)


def _gate_up(stream, wg, wu, layer, *, tm, tn, name):
    hg, ss = stream
    m, k = hg.shape
    n = wg.shape[2]
    return pl.pallas_call(
        _gate_up_kernel,
        grid=(m // tm, n // tn),
        in_specs=[_panel_spec(tm, k), _ss_spec(tm),
                  _weight_spec(layer, k, tn), _weight_spec(layer, k, tn)],
        out_specs=pl.BlockSpec((tm, tn), lambda i, j: (i, j)),
        out_shape=jax.ShapeDtypeStruct((m, n), BF16),
        compiler_params=_params(("parallel", "arbitrary")),
        name=name,
    )(hg, ss, wg, wu)


def _residual_kernel(x_ref, w_ref, h_ref, o_ref, *, scale):
    y = jnp.dot(x_ref[...], w_ref[...].astype(BF16), preferred_element_type=F32)
    o_ref[...] = h_ref[...] + scale * y


def _residual_prenorm_kernel(x_ref, w_ref, h_ref, g_ref, o_ref, hg_ref, ss_ref,
                             *, scale):
    y = jnp.dot(x_ref[...], w_ref[...].astype(BF16), preferred_element_type=F32)
    y = h_ref[...] + scale * y
    o_ref[...] = y
    hg_ref[...] = (y * g_ref[...]).astype(hg_ref.dtype)
    part = _lane_partial_sumsq(y)
    j = pl.program_id(1)

    @pl.when(j == 0)
    def _():
        ss_ref[...] = part

    @pl.when(j > 0)
    def _():
        ss_ref[...] += part


def _residual_matmul(x, w, layer, h, *, tm, tn, scale, name, next_gain=None):
    m, k = x.shape
    n = w.shape[2]
    tile = pl.BlockSpec((tm, tn), lambda i, j: (i, j))
    in_specs = [_panel_spec(tm, k), _weight_spec(layer, k, tn), tile]
    h_shape = jax.ShapeDtypeStruct((m, n), F32)
    if next_gain is None:
        return pl.pallas_call(
            functools.partial(_residual_kernel, scale=scale),
            grid=(m // tm, n // tn),
            in_specs=in_specs,
            out_specs=tile,
            out_shape=h_shape,
            compiler_params=_params(("parallel", "arbitrary")),
            name=name,
        )(x, w, h), None
    h_new, hg, ss = pl.pallas_call(
        functools.partial(_residual_prenorm_kernel, scale=scale),
        grid=(m // tm, n // tn),
        in_specs=in_specs + [pl.BlockSpec((1, tn), lambda i, j: (0, j))],
        out_specs=[tile, tile, _ss_spec(tm)],
        out_shape=[h_shape, jax.ShapeDtypeStruct((m, n), BF16),
                   jax.ShapeDtypeStruct((m, LANES), F32)],
        compiler_params=_params(("parallel", "arbitrary")),
        name=name,
    )(x, w, h, next_gain.reshape(1, n))
    return h_new, (hg, ss)


def _ffn(h, stream, wg, wu, wd, layer, next_gain, *, name):
    act = _gate_up(stream, wg, wu, layer, tm=ROWS_LARGE, tn=FFN_COLS,
                   name=name + "_gate_up")
    return _residual_matmul(act, wd, layer, h, tm=ROWS_MEDIUM, tn=FFN_COLS,
                            scale=MACARON_WEIGHT, next_gain=next_gain,
                            name=name + "_down")


CONV_ROWS = 512
CONV_COLS = 512
CONV_COL_BLOCKS = D_MODEL // CONV_COLS
CONV_TILES_PER_BATCH = SEQ // CONV_ROWS


def _conv_taps(u, u_left, u_right, b_gate, w_ref, bias_ref, o_ref):
    conv = (u_left * w_ref[0:1, :] + u * w_ref[1:2, :] + u_right * w_ref[2:3, :]
            + bias_ref[...])
    o_ref[...] = (b_gate * conv).astype(o_ref.dtype)


def _conv_real_kernel(b_ref, c_ref, h_ref, cp_ref, hp_ref, cn_ref, hn_ref,
                      w_ref, bias_ref, o_ref):
    rows = c_ref.shape[0]
    u = c_ref[...] * h_ref[...]
    u_prev = cp_ref[SUBLANES - 1:SUBLANES, :] * hp_ref[SUBLANES - 1:SUBLANES, :]
    last_in_batch = (pl.program_id(0) % CONV_TILES_PER_BATCH
                     == CONV_TILES_PER_BATCH - 1)
    u_next = jnp.where(last_in_batch, 0.0, cn_ref[0:1, :] * hn_ref[0:1, :])
    row = lax.broadcasted_iota(jnp.int32, u.shape, 0)
    u_left = jnp.where(row == 0, u_prev, pltpu.roll(u, 1, 0))
    u_right = jnp.where(row == rows - 1, u_next, pltpu.roll(u, rows - 1, 0))
    _conv_taps(u, u_left, u_right, b_ref[...], w_ref, bias_ref, o_ref)


def _conv_meta_kernel(b_ref, c_ref, h_ref, cn_ref, hn_ref, w_ref, bias_ref,
                      y_hbm_ref, o_ref):
    del y_hbm_ref
    u = c_ref[...] * h_ref[...]
    u_next = cn_ref[0:1, :] * hn_ref[0:1, :]
    row = lax.broadcasted_iota(jnp.int32, u.shape, 0)
    u_left = jnp.where(row == 0, 0.0, pltpu.roll(u, 1, 0))
    u_right = jnp.where(row == N_META - 1, u_next, pltpu.roll(u, N_META - 1, 0))
    _conv_taps(u, u_left, u_right, b_ref[...], w_ref, bias_ref, o_ref)


def _conv_gate(z, conv_w, conv_b):
    nb = CONV_COL_BLOCKS
    halo_per_tile = CONV_ROWS // SUBLANES
    meta_halo0 = (M_REAL + N_META - SUBLANES) // SUBLANES
    meta_halo_step = N_META // SUBLANES

    def prev_block(i):
        first = i % CONV_TILES_PER_BATCH == 0
        batch = i // CONV_TILES_PER_BATCH
        return jnp.where(first, meta_halo0 + meta_halo_step * batch,
                         i * halo_per_tile - 1)

    def next_block(i):
        last = i % CONV_TILES_PER_BATCH == CONV_TILES_PER_BATCH - 1
        return jnp.where(last, i * halo_per_tile, (i + 1) * halo_per_tile)

    def tile(col0):
        return pl.BlockSpec((CONV_ROWS, CONV_COLS), lambda i, j: (i, j + col0))

    def halo(block_fn, col0):
        return pl.BlockSpec((SUBLANES, CONV_COLS),
                            lambda i, j: (block_fn(i), j + col0))

    w_spec = pl.BlockSpec((3, CONV_COLS), lambda i, j: (0, j))
    bias_spec = pl.BlockSpec((1, CONV_COLS), lambda i, j: (0, j))
    bias = conv_b.reshape(1, D_MODEL)
    y = pl.pallas_call(
        _conv_real_kernel,
        grid=(M_REAL // CONV_ROWS, nb),
        in_specs=[tile(0), tile(nb), tile(2 * nb),
                  halo(prev_block, nb), halo(prev_block, 2 * nb),
                  halo(next_block, nb), halo(next_block, 2 * nb),
                  w_spec, bias_spec],
        out_specs=pl.BlockSpec((CONV_ROWS, CONV_COLS), lambda i, j: (i, j)),
        out_shape=jax.ShapeDtypeStruct((M_ROWS, D_MODEL), BF16),
        compiler_params=_params(("parallel", "parallel")),
        name="conv_real",
    )(z, z, z, z, z, z, z, conv_w, bias)

    meta0 = M_REAL // N_META
    real0_step = SEQ // SUBLANES

    def meta_tile(col0):
        return pl.BlockSpec((N_META, CONV_COLS), lambda b, j: (meta0 + b, j + col0))

    def first_real(col0):
        return pl.BlockSpec((SUBLANES, CONV_COLS),
                            lambda b, j: (b * real0_step, j + col0))

    w_spec = pl.BlockSpec((3, CONV_COLS), lambda b, j: (0, j))
    bias_spec = pl.BlockSpec((1, CONV_COLS), lambda b, j: (0, j))
    return pl.pallas_call(
        _conv_meta_kernel,
        grid=(BATCH, nb),
        in_specs=[meta_tile(0), meta_tile(nb), meta_tile(2 * nb),
                  first_real(nb), first_real(2 * nb), w_spec, bias_spec,
                  pl.BlockSpec(memory_space=pl.ANY)],
        out_specs=pl.BlockSpec((N_META, CONV_COLS), lambda b, j: (meta0 + b, j)),
        out_shape=jax.ShapeDtypeStruct((M_ROWS, D_MODEL), BF16),
        input_output_aliases={7: 0},
        compiler_params=_params(("parallel", "parallel")),
        name="conv_meta",
    )(z, z, z, z, z, conv_w, bias, y)


def _conv_mixer(h, stream, w_in, conv_w, conv_b, w_out, layer, next_gain):
    z = _normed_matmul(stream, w_in, layer, tm=ROWS_LARGE, tn=PROJ_COLS,
                       out_dtype=F32, name="conv_in")
    y = _conv_gate(z, conv_w[layer], conv_b[layer])
    return _residual_matmul(y, w_out, layer, h, tm=ROWS_LARGE, tn=PROJ_COLS,
                            scale=1.0, next_gain=next_gain, name="conv_out")


def _rope_tables():
    t = jnp.arange(SEQ, dtype=jnp.int32)
    real_row = (t // GRID_W).astype(F32)
    real_col = (t % GRID_W).astype(F32)
    meta_row = jnp.full((N_META,), -1.0, F32)
    meta_col = jnp.arange(N_META, dtype=F32)
    row = jnp.concatenate([jnp.tile(real_row, BATCH), jnp.tile(meta_row, BATCH)])
    col = jnp.concatenate([jnp.tile(real_col, BATCH), jnp.tile(meta_col, BATCH)])
    inv_freq = ROPE_THETA ** (-jnp.arange(0, ROPE_AXIS_DIM, 2, dtype=F32)
                              / ROPE_AXIS_DIM)
    ar = row[:, None] * inv_freq
    ac = col[:, None] * inv_freq
    zero = jnp.zeros_like(ar)
    cos = jnp.concatenate([jnp.cos(ar), jnp.cos(ar), jnp.cos(ac), jnp.cos(ac)], -1)
    sin_lo = jnp.concatenate([-jnp.sin(ar), zero, -jnp.sin(ac), zero], -1)
    sin_hi = jnp.concatenate([zero, jnp.sin(ar), zero, jnp.sin(ac)], -1)
    return cos, sin_lo, sin_hi


def _qk_prep_kernel(qkv_ref, cos_ref, slo_ref, shi_ref, qg_ref, kg_ref,
                    q_ref, k_ref, v_ref):
    cos = cos_ref[...]
    slo = slo_ref[...]
    shi = shi_ref[...]
    half = ROPE_AXIS_DIM // 2

    def prep(x, gain):
        ms = jnp.mean(x * x, axis=-1, keepdims=True)
        y = x * lax.rsqrt(ms + NORM_EPS) * gain
        return (y * cos + pltpu.roll(y, HEAD_DIM - half, 1) * slo
                + pltpu.roll(y, half, 1) * shi)

    q_gain = qg_ref[...] * (HEAD_DIM ** -0.5 * LOG2_E)
    k_gain = kg_ref[...]
    for hd in range(N_HEADS):
        sl = slice(hd * HEAD_DIM, (hd + 1) * HEAD_DIM)
        q_ref[:, sl] = prep(qkv_ref[:, sl], q_gain).astype(q_ref.dtype)
    for hd in range(N_KV_HEADS):
        sl = slice(hd * HEAD_DIM, (hd + 1) * HEAD_DIM)
        src = slice(Q_WIDTH + hd * HEAD_DIM, Q_WIDTH + (hd + 1) * HEAD_DIM)
        k_ref[:, sl] = prep(qkv_ref[:, src], k_gain).astype(k_ref.dtype)
    v_ref[...] = qkv_ref[:, Q_WIDTH + KV_WIDTH:].astype(v_ref.dtype)


def _qk_prep(qkv, q_gain, k_gain, *, block_rows):
    cos, slo, shi = _rope_tables()
    rows = lambda w: pl.BlockSpec((block_rows, w), lambda i: (i, 0))
    gain = pl.BlockSpec((1, HEAD_DIM), lambda i: (0, 0))
    return pl.pallas_call(
        _qk_prep_kernel,
        grid=(M_ROWS // block_rows,),
        in_specs=[rows(QKV_WIDTH), rows(HEAD_DIM), rows(HEAD_DIM), rows(HEAD_DIM),
                  gain, gain],
        out_specs=[rows(Q_WIDTH), rows(KV_WIDTH), rows(KV_WIDTH)],
        out_shape=[jax.ShapeDtypeStruct((M_ROWS, Q_WIDTH), BF16),
                   jax.ShapeDtypeStruct((M_ROWS, KV_WIDTH), BF16),
                   jax.ShapeDtypeStruct((M_ROWS, KV_WIDTH), BF16)],
        compiler_params=_params(("parallel",)),
        name="qk_prep",
    )(qkv, cos, slo, shi, q_gain.reshape(1, HEAD_DIM), k_gain.reshape(1, HEAD_DIM))


ATTN_KV_CHUNK = 512
ATTN_STRIP_ROWS = 32
META_PAD = 128

_NT = (((1,), (1,)), ((), ()))


def _flash_kernel(q_ref, k_ref, v_ref, km_ref, vm_ref, *rest, tq):
    (o_ref, q_all, s0, s1, p0, p1, a0, a1, m_ref, l_ref, acc_ref) = rest[-11:]
    s_buf, p_buf, a_buf = (s0, s1), (p0, p1), (a0, a1)
    nq = GROUP * tq
    strip = min(ATTN_STRIP_ROWS, tq)
    n_chunks = SEQ // ATTN_KV_CHUNK

    for g in range(GROUP):
        q_all[g * tq:(g + 1) * tq, :] = q_ref[:, g * HEAD_DIM:(g + 1) * HEAD_DIM]
    m_ref[...] = jnp.full(m_ref.shape, -jnp.inf, F32)
    l_ref[...] = jnp.zeros(l_ref.shape, F32)

    def scores(kc, buf, width):
        s_buf[buf][:, :width] = lax.dot_general(q_all[...], kc, _NT,
                                                preferred_element_type=F32)

    def softmax(buf, width, n_valid):
        for r in range(nq // strip):
            rs = slice(r * strip, (r + 1) * strip)
            s = s_buf[buf][rs, :width]
            if n_valid < width:
                col = lax.broadcasted_iota(jnp.int32, s.shape, 1)
                s = jnp.where(col < n_valid, s, -jnp.inf)
            m_old = m_ref[rs, :]
            m_new = jnp.maximum(m_old, jnp.max(s, axis=-1, keepdims=True))
            alpha = jnp.exp2(m_old - m_new)
            l_new = alpha * l_ref[rs, :]
            for j in range(width // LANES):
                cols = slice(j * LANES, (j + 1) * LANES)
                p = jnp.exp2(s[:, cols] - m_new)
                l_new = l_new + p
                p_buf[buf][rs, cols] = p.astype(BF16)
            l_ref[rs, :] = l_new
            m_ref[rs, :] = m_new
            a_buf[buf][rs, :] = alpha

    def values(vc, buf, width):
        pv = jnp.dot(p_buf[buf][:, :width], vc, preferred_element_type=F32)
        acc_ref[...] = a_buf[buf][...] * acc_ref[...] + pv

    def k_chunk(c):
        return k_ref[pl.ds(pl.multiple_of(c * ATTN_KV_CHUNK, ATTN_KV_CHUNK),
                           ATTN_KV_CHUNK), :]

    def v_chunk(c):
        return v_ref[pl.ds(pl.multiple_of(c * ATTN_KV_CHUNK, ATTN_KV_CHUNK),
                           ATTN_KV_CHUNK), :]

    pad = jnp.zeros((META_PAD - N_META, HEAD_DIM), BF16)
    scores(jnp.concatenate([km_ref[...], pad], axis=0), 1, META_PAD)
    scores(k_chunk(0), 0, ATTN_KV_CHUNK)
    softmax(1, META_PAD, N_META)
    acc_ref[...] = jnp.dot(p_buf[1][:, :META_PAD],
                           jnp.concatenate([vm_ref[...], pad], axis=0),
                           preferred_element_type=F32)
    scores(k_chunk(1), 1, ATTN_KV_CHUNK)
    softmax(0, ATTN_KV_CHUNK, ATTN_KV_CHUNK)

    def body(i, carry):
        c = 2 * i + 1
        values(v_chunk(c - 1), 0, ATTN_KV_CHUNK)
        scores(k_chunk(c + 1), 0, ATTN_KV_CHUNK)
        softmax(1, ATTN_KV_CHUNK, ATTN_KV_CHUNK)
        values(v_chunk(c), 1, ATTN_KV_CHUNK)
        scores(k_chunk(c + 2), 1, ATTN_KV_CHUNK)
        softmax(0, ATTN_KV_CHUNK, ATTN_KV_CHUNK)
        return carry

    lax.fori_loop(0, (n_chunks - 2) // 2, body, 0)
    values(v_chunk(n_chunks - 2), 0, ATTN_KV_CHUNK)
    softmax(1, ATTN_KV_CHUNK, ATTN_KV_CHUNK)
    values(v_chunk(n_chunks - 1), 1, ATTN_KV_CHUNK)

    for g in range(GROUP):
        rows = slice(g * tq, (g + 1) * tq)
        denom = jnp.sum(l_ref[rows, :], axis=-1, keepdims=True)
        o_ref[:, g * HEAD_DIM:(g + 1) * HEAD_DIM] = (
            acc_ref[rows, :] / denom).astype(o_ref.dtype)


def _flash_scratch(tq):
    nq = GROUP * tq
    scores = pltpu.VMEM((nq, ATTN_KV_CHUNK), F32)
    probs = pltpu.VMEM((nq, ATTN_KV_CHUNK), BF16)
    stat = pltpu.VMEM((nq, LANES), F32)
    return [pltpu.VMEM((nq, HEAD_DIM), BF16),
            scores, scores, probs, probs,
            stat, stat,
            stat,
            stat,
            pltpu.VMEM((nq, HEAD_DIM), F32)]


def _attention(q, k, v, *, tq):
    qw = GROUP * HEAD_DIM
    n_q = SEQ // tq
    meta0 = M_REAL // N_META
    kv_real = pl.BlockSpec((SEQ, HEAD_DIM), lambda b, g, i: (b, g))
    kv_meta = pl.BlockSpec((N_META, HEAD_DIM), lambda b, g, i: (meta0 + b, g))
    q_spec = pl.BlockSpec((tq, qw), lambda b, g, i: (b * n_q + i, g))
    o_real = pl.pallas_call(
        functools.partial(_flash_kernel, tq=tq),
        grid=(BATCH, N_KV_HEADS, n_q),
        in_specs=[q_spec, kv_real, kv_real, kv_meta, kv_meta],
        out_specs=q_spec,
        out_shape=jax.ShapeDtypeStruct((M_ROWS, Q_WIDTH), BF16),
        scratch_shapes=_flash_scratch(tq),
        compiler_params=_params(("parallel", "parallel", "arbitrary")),
        name="attn_real",
    )(q, k, v, k, v)

    qm_spec = pl.BlockSpec((N_META, qw), lambda b, g, i: (meta0 + b, g))
    return pl.pallas_call(
        functools.partial(_flash_kernel, tq=N_META),
        grid=(BATCH, N_KV_HEADS, 1),
        in_specs=[qm_spec, kv_real, kv_real, kv_meta, kv_meta,
                  pl.BlockSpec(memory_space=pl.ANY)],
        out_specs=qm_spec,
        out_shape=jax.ShapeDtypeStruct((M_ROWS, Q_WIDTH), BF16),
        input_output_aliases={5: 0},
        scratch_shapes=_flash_scratch(N_META),
        compiler_params=_params(("parallel", "parallel", "arbitrary")),
        name="attn_meta",
    )(q, k, v, k, v, o_real)


def _attn_mixer(h, stream, w_qkv, q_gain, k_gain, w_o, layer, next_gain):
    qkv = _normed_matmul(stream, w_qkv, layer, tm=ROWS_LARGE, tn=PROJ_COLS,
                         out_dtype=F32, name="attn_qkv")
    q, k, v = _qk_prep(qkv, q_gain[layer], k_gain[layer], block_rows=ROWS_SMALL)
    o = _attention(q, k, v, tq=ATTN_Q_ROWS)
    return _residual_matmul(o, w_o, layer, h, tm=ROWS_LARGE, tn=PROJ_COLS,
                            scale=1.0, next_gain=next_gain, name="attn_out")


def kernel(x, meta_tokens, ffn_a_norm, ffn_a_w_gate, ffn_a_w_up, ffn_a_w_down, ffn_b_norm, ffn_b_w_gate, ffn_b_w_up, ffn_b_w_down, conv_norm, conv_w_in, conv_w, conv_b, conv_w_out, attn_norm, attn_w_qkv, attn_q_norm, attn_k_norm, attn_w_o, final_norm):
    bsz, n_real, d = x.shape
    assert (bsz, n_real, d) == (BATCH, SEQ, D_MODEL)
    meta = meta_tokens.astype(x.dtype)
    h = jnp.concatenate([x.reshape(M_REAL, D_MODEL)] + [meta] * BATCH, axis=0)
    stream = _prenorm(h, ffn_a_norm[0], block_rows=ROWS_SMALL, name="embed_norm")

    for i in range(DEPTH):
        j = i // 2
        is_conv = i % 2 == 0
        mixer_gain = conv_norm[j] if is_conv else attn_norm[j]
        h, stream = _ffn(h, stream, ffn_a_w_gate, ffn_a_w_up, ffn_a_w_down, i,
                         mixer_gain, name=f"ffn_a{i}")
        if is_conv:
            h, stream = _conv_mixer(h, stream, conv_w_in, conv_w, conv_b,
                                    conv_w_out, j, ffn_b_norm[i])
        else:
            h, stream = _attn_mixer(h, stream, attn_w_qkv, attn_q_norm,
                                    attn_k_norm, attn_w_o, j, ffn_b_norm[i])
        after = ffn_a_norm[i + 1] if i + 1 < DEPTH else None
        h, stream = _ffn(h, stream, ffn_b_w_gate, ffn_b_w_up, ffn_b_w_down, i,
                         after, name=f"ffn_b{i}")

    out = _rmsnorm(h, final_norm, rows=M_REAL, out_dtype=F32, block_rows=FINAL_ROWS,
                   name="final_norm")
    return out.reshape(BATCH, SEQ, D_MODEL)
```

```python
import functools

import jax
import jax.numpy as jnp
from jax import lax
from jax.experimental import pallas as pl
from jax.experimental.pallas import tpu as pltpu

D_MODEL = 4096
BATCH = 2
SEQ = 8192
DEPTH = 2
N_META = 16
GRID_W = 64
D_FF = 11008
N_HEADS = 32
N_KV_HEADS = 8
HEAD_DIM = 128
GROUP = N_HEADS // N_KV_HEADS
ROPE_AXIS_DIM = HEAD_DIM // 2
ROPE_THETA = 10000.0
NORM_EPS = 1e-6
LOG2_E = 1.4426950408889634
MACARON_WEIGHT = 0.5

M_REAL = BATCH * SEQ
M_META = BATCH * N_META
M_ROWS = M_REAL + M_META
Q_WIDTH = N_HEADS * HEAD_DIM
KV_WIDTH = N_KV_HEADS * HEAD_DIM
QKV_WIDTH = Q_WIDTH + 2 * KV_WIDTH

VMEM_LIMIT_BYTES = 56 * 1024 * 1024
SUBLANES = 8
LANES = 128

ROWS_XLARGE = 2736
ROWS_LARGE = 1824
ROWS_MEDIUM = 912
ROWS_SMALL = 304
FINAL_ROWS = 256
ATTN_Q_ROWS = 256
FFN_COLS = 256
PROJ_COLS = 512

F32 = jnp.float32
BF16 = jnp.bfloat16


def _params(semantics):
    return pltpu.CompilerParams(dimension_semantics=semantics,
                                vmem_limit_bytes=VMEM_LIMIT_BYTES)


def _lane_partial_sumsq(y):
    sq = y * y
    part = sq[:, :LANES]
    for c in range(1, y.shape[1] // LANES):
        part = part + sq[:, c * LANES:(c + 1) * LANES]
    return part


def _row_scale(ss_ref):
    total = jnp.sum(ss_ref[...], axis=-1, keepdims=True)
    return lax.rsqrt(total * (1.0 / D_MODEL) + NORM_EPS)


def _prenorm_kernel(h_ref, g_ref, hg_ref, ss_ref):
    x = h_ref[...]
    hg_ref[...] = (x * g_ref[...]).astype(hg_ref.dtype)
    ss_ref[...] = _lane_partial_sumsq(x)


def _prenorm(h, gain, *, block_rows, name):
    rows = h.shape[0]
    row_spec = lambda w: pl.BlockSpec((block_rows, w), lambda i: (i, 0))
    return pl.pallas_call(
        _prenorm_kernel,
        grid=(rows // block_rows,),
        in_specs=[row_spec(D_MODEL), pl.BlockSpec((1, D_MODEL), lambda i: (0, 0))],
        out_specs=[row_spec(D_MODEL), row_spec(LANES)],
        out_shape=[jax.ShapeDtypeStruct((rows, D_MODEL), BF16),
                   jax.ShapeDtypeStruct((rows, LANES), F32)],
        compiler_params=_params(("parallel",)),
        name=name,
    )(h, gain.reshape(1, D_MODEL))


def _rmsnorm_kernel(h_ref, g_ref, o_ref):
    x = h_ref[...]
    ms = jnp.mean(x * x, axis=-1, keepdims=True)
    o_ref[...] = (x * lax.rsqrt(ms + NORM_EPS) * g_ref[...]).astype(o_ref.dtype)


def _rmsnorm(h, gain, *, rows, out_dtype, block_rows, name):
    return pl.pallas_call(
        _rmsnorm_kernel,
        grid=(rows // block_rows,),
        in_specs=[pl.BlockSpec((block_rows, D_MODEL), lambda i: (i, 0)),
                  pl.BlockSpec((1, D_MODEL), lambda i: (0, 0))],
        out_specs=pl.BlockSpec((block_rows, D_MODEL), lambda i: (i, 0)),
        out_shape=jax.ShapeDtypeStruct((rows, D_MODEL), out_dtype),
        compiler_params=_params(("parallel",)),
        name=name,
    )(h, gain.reshape(1, D_MODEL))


def _panel_spec(tm, k):
    return pl.BlockSpec((tm, k), lambda i, j: (i, 0), pipeline_mode=pl.Buffered(1))


def _weight_spec(layer, k, tn):
    return pl.BlockSpec((None, k, tn), lambda i, j: (layer, 0, j))


def _ss_spec(tm):
    return pl.BlockSpec((tm, LANES), lambda i, j: (i, 0))


def _normed_mm_kernel(x_ref, ss_ref, w_ref, o_ref):
    y = jnp.dot(x_ref[...], w_ref[...].astype(BF16), preferred_element_type=F32)
    o_ref[...] = (_row_scale(ss_ref) * y).astype(o_ref.dtype)


def _normed_matmul(stream, w, layer, *, tm, tn, out_dtype, name):
    hg, ss = stream
    m, k = hg.shape
    n = w.shape[2]
    return pl.pallas_call(
        _normed_mm_kernel,
        grid=(m // tm, n // tn),
        in_specs=[_panel_spec(tm, k), _ss_spec(tm), _weight_spec(layer, k, tn)],
        out_specs=pl.BlockSpec((tm, tn), lambda i, j: (i, j)),
        out_shape=jax.ShapeDtypeStruct((m, n), out_dtype),
        compiler_params=_params(("parallel", "arbitrary")),
        name=name,
    )(hg, ss, w)


def _gate_up_kernel(x_ref, ss_ref, wg_ref, wu_ref, o_ref):
    x = x_ref[...]
    r = _row_scale(ss_ref)
    g = r * jnp.dot(x, wg_ref[...].astype(BF16), preferred_element_type=F32)
    u = r * jnp.dot(x, wu_ref[...].astype(BF16), preferred_element_type=F32)
    o_ref[...] = (g * jax.nn.sigmoid(g) * u).astype(o_ref.dtype)


def _gate_up(stream, wg, wu, layer, *, tm, tn, name):
    hg, ss = stream
    m, k = hg.shape
    n = wg.shape[2]
    return pl.pallas_call(
        _gate_up_kernel,
        grid=(m // tm, n // tn),
        in_specs=[_panel_spec(tm, k), _ss_spec(tm),
                  _weight_spec(layer, k, tn), _weight_spec(layer, k, tn)],
        out_specs=pl.BlockSpec((tm, tn), lambda i, j: (i, j)),
        out_shape=jax.ShapeDtypeStruct((m, n), BF16),
        compiler_params=_params(("parallel", "arbitrary")),
        name=name,
    )(hg, ss, wg, wu)


def _residual_kernel(x_ref, w_ref, h_ref, o_ref, *, scale):
    y = jnp.dot(x_ref[...], w_ref[...].astype(BF16), preferred_element_type=F32)
    o_ref[...] = h_ref[...] + scale * y


def _residual_prenorm_kernel(x_ref, w_ref, h_ref, g_ref, o_ref, hg_ref, ss_ref,
                             *, scale):
    y = jnp.dot(x_ref[...], w_ref[...].astype(BF16), preferred_element_type=F32)
    y = h_ref[...] + scale * y
    o_ref[...] = y
    hg_ref[...] = (y * g_ref[...]).astype(hg_ref.dtype)
    part = _lane_partial_sumsq(y)
    j = pl.program_id(1)

    @pl.when(j == 0)
    def _():
        ss_ref[...] = part

    @pl.when(j > 0)
    def _():
        ss_ref[...] += part


def _residual_matmul(x, w, layer, h, *, tm, tn, scale, name, next_gain=None):
    m, k = x.shape
    n = w.shape[2]
    tile = pl.BlockSpec((tm, tn), lambda i, j: (i, j))
    in_specs = [_panel_spec(tm, k), _weight_spec(layer, k, tn), tile]
    h_shape = jax.ShapeDtypeStruct((m, n), F32)
    if next_gain is None:
        return pl.pallas_call(
            functools.partial(_residual_kernel, scale=scale),
            grid=(m // tm, n // tn),
            in_specs=in_specs,
            out_specs=tile,
            out_shape=h_shape,
            compiler_params=_params(("parallel", "arbitrary")),
            name=name,
        )(x, w, h), None
    h_new, hg, ss = pl.pallas_call(
        functools.partial(_residual_prenorm_kernel, scale=scale),
        grid=(m // tm, n // tn),
        in_specs=in_specs + [pl.BlockSpec((1, tn), lambda i, j: (0, j))],
        out_specs=[tile, tile, _ss_spec(tm)],
        out_shape=[h_shape, jax.ShapeDtypeStruct((m, n), BF16),
                   jax.ShapeDtypeStruct((m, LANES), F32)],
        compiler_params=_params(("parallel", "arbitrary")),
        name=name,
    )(x, w, h, next_gain.reshape(1, n))
    return h_new, (hg, ss)


def _ffn(h, stream, wg, wu, wd, layer, next_gain, *, name):
    act = _gate_up(stream, wg, wu, layer, tm=ROWS_XLARGE, tn=FFN_COLS,
                   name=name + "_gate_up")
    return _residual_matmul(act, wd, layer, h, tm=ROWS_MEDIUM, tn=FFN_COLS,
                            scale=MACARON_WEIGHT, next_gain=next_gain,
                            name=name + "_down")


CONV_ROWS = 512
CONV_COLS = 512
CONV_COL_BLOCKS = D_MODEL // CONV_COLS
CONV_TILES_PER_BATCH = SEQ // CONV_ROWS


def _conv_taps(u, u_left, u_right, b_gate, w_ref, bias_ref, o_ref):
    conv = (u_left * w_ref[0:1, :] + u * w_ref[1:2, :] + u_right * w_ref[2:3, :]
            + bias_ref[...])
    o_ref[...] = (b_gate * conv).astype(o_ref.dtype)


def _conv_real_kernel(b_ref, c_ref, h_ref, cp_ref, hp_ref, cn_ref, hn_ref,
                      w_ref, bias_ref, o_ref):
    rows = c_ref.shape[0]
    u = c_ref[...] * h_ref[...]
    u_prev = cp_ref[SUBLANES - 1:SUBLANES, :] * hp_ref[SUBLANES - 1:SUBLANES, :]
    last_in_batch = (pl.program_id(0) % CONV_TILES_PER_BATCH
                     == CONV_TILES_PER_BATCH - 1)
    u_next = jnp.where(last_in_batch, 0.0, cn_ref[0:1, :] * hn_ref[0:1, :])
    row = lax.broadcasted_iota(jnp.int32, u.shape, 0)
    u_left = jnp.where(row == 0, u_prev, pltpu.roll(u, 1, 0))
    u_right = jnp.where(row == rows - 1, u_next, pltpu.roll(u, rows - 1, 0))
    _conv_taps(u, u_left, u_right, b_ref[...], w_ref, bias_ref, o_ref)


def _conv_meta_kernel(b_ref, c_ref, h_ref, cn_ref, hn_ref, w_ref, bias_ref,
                      y_hbm_ref, o_ref):
    del y_hbm_ref
    u = c_ref[...] * h_ref[...]
    u_next = cn_ref[0:1, :] * hn_ref[0:1, :]
    row = lax.broadcasted_iota(jnp.int32, u.shape, 0)
    u_left = jnp.where(row == 0, 0.0, pltpu.roll(u, 1, 0))
    u_right = jnp.where(row == N_META - 1, u_next, pltpu.roll(u, N_META - 1, 0))
    _conv_taps(u, u_left, u_right, b_ref[...], w_ref, bias_ref, o_ref)


def _conv_gate(z, conv_w, conv_b):
    nb = CONV_COL_BLOCKS
    halo_per_tile = CONV_ROWS // SUBLANES
    meta_halo0 = (M_REAL + N_META - SUBLANES) // SUBLANES
    meta_halo_step = N_META // SUBLANES

    def prev_block(i):
        first = i % CONV_TILES_PER_BATCH == 0
        batch = i // CONV_TILES_PER_BATCH
        return jnp.where(first, meta_halo0 + meta_halo_step * batch,
                         i * halo_per_tile - 1)

    def next_block(i):
        last = i % CONV_TILES_PER_BATCH == CONV_TILES_PER_BATCH - 1
        return jnp.where(last, i * halo_per_tile, (i + 1) * halo_per_tile)

    def tile(col0):
        return pl.BlockSpec((CONV_ROWS, CONV_COLS), lambda i, j: (i, j + col0))

    def halo(block_fn, col0):
        return pl.BlockSpec((SUBLANES, CONV_COLS),
                            lambda i, j: (block_fn(i), j + col0))

    w_spec = pl.BlockSpec((3, CONV_COLS), lambda i, j: (0, j))
    bias_spec = pl.BlockSpec((1, CONV_COLS), lambda i, j: (0, j))
    bias = conv_b.reshape(1, D_MODEL)
    y = pl.pallas_call(
        _conv_real_kernel,
        grid=(M_REAL // CONV_ROWS, nb),
        in_specs=[tile(0), tile(nb), tile(2 * nb),
                  halo(prev_block, nb), halo(prev_block, 2 * nb),
                  halo(next_block, nb), halo(next_block, 2 * nb),
                  w_spec, bias_spec],
        out_specs=pl.BlockSpec((CONV_ROWS, CONV_COLS), lambda i, j: (i, j)),
        out_shape=jax.ShapeDtypeStruct((M_ROWS, D_MODEL), BF16),
        compiler_params=_params(("parallel", "parallel")),
        name="conv_real",
    )(z, z, z, z, z, z, z, conv_w, bias)

    meta0 = M_REAL // N_META
    real0_step = SEQ // SUBLANES

    def meta_tile(col0):
        return pl.BlockSpec((N_META, CONV_COLS), lambda b, j: (meta0 + b, j + col0))

    def first_real(col0):
        return pl.BlockSpec((SUBLANES, CONV_COLS),
                            lambda b, j: (b * real0_step, j + col0))

    w_spec = pl.BlockSpec((3, CONV_COLS), lambda b, j: (0, j))
    bias_spec = pl.BlockSpec((1, CONV_COLS), lambda b, j: (0, j))
    return pl.pallas_call(
        _conv_meta_kernel,
        grid=(BATCH, nb),
        in_specs=[meta_tile(0), meta_tile(nb), meta_tile(2 * nb),
                  first_real(nb), first_real(2 * nb), w_spec, bias_spec,
                  pl.BlockSpec(memory_space=pl.ANY)],
        out_specs=pl.BlockSpec((N_META, CONV_COLS), lambda b, j: (meta0 + b, j)),
        out_shape=jax.ShapeDtypeStruct((M_ROWS, D_MODEL), BF16),
        input_output_aliases={7: 0},
        compiler_params=_params(("parallel", "parallel")),
        name="conv_meta",
    )(z, z, z, z, z, conv_w, bias, y)


def _conv_mixer(h, stream, w_in, conv_w, conv_b, w_out, layer, next_gain):
    z = _normed_matmul(stream, w_in, layer, tm=ROWS_LARGE, tn=PROJ_COLS,
                       out_dtype=F32, name="conv_in")
    y = _conv_gate(z, conv_w[layer], conv_b[layer])
    return _residual_matmul(y, w_out, layer, h, tm=ROWS_LARGE, tn=PROJ_COLS,
                            scale=1.0, next_gain=next_gain, name="conv_out")


def _rope_tables():
    t = jnp.arange(SEQ, dtype=jnp.int32)
    real_row = (t // GRID_W).astype(F32)
    real_col = (t % GRID_W).astype(F32)
    meta_row = jnp.full((N_META,), -1.0, F32)
    meta_col = jnp.arange(N_META, dtype=F32)
    row = jnp.concatenate([jnp.tile(real_row, BATCH), jnp.tile(meta_row, BATCH)])
    col = jnp.concatenate([jnp.tile(real_col, BATCH), jnp.tile(meta_col, BATCH)])
    inv_freq = ROPE_THETA ** (-jnp.arange(0, ROPE_AXIS_DIM, 2, dtype=F32)
                              / ROPE_AXIS_DIM)
    ar = row[:, None] * inv_freq
    ac = col[:, None] * inv_freq
    zero = jnp.zeros_like(ar)
    cos = jnp.concatenate([jnp.cos(ar), jnp.cos(ar), jnp.cos(ac), jnp.cos(ac)], -1)
    sin_lo = jnp.concatenate([-jnp.sin(ar), zero, -jnp.sin(ac), zero], -1)
    sin_hi = jnp.concatenate([zero, jnp.sin(ar), zero, jnp.sin(ac)], -1)
    return cos, sin_lo, sin_hi


def _qkv_kernel(x_ref, ss_ref, w_ref, cos_ref, slo_ref, shi_ref, qg_ref, kg_ref,
                o_ref):
    j = pl.program_id(1)
    y = _row_scale(ss_ref) * jnp.dot(x_ref[...], w_ref[...].astype(BF16),
                                     preferred_element_type=F32)
    half = ROPE_AXIS_DIM // 2

    @pl.when(j < (Q_WIDTH + KV_WIDTH) // PROJ_COLS)
    def _():
        cos = cos_ref[...]
        slo = slo_ref[...]
        shi = shi_ref[...]
        gain = jnp.where(j < Q_WIDTH // PROJ_COLS,
                         qg_ref[...] * (HEAD_DIM ** -0.5 * LOG2_E), kg_ref[...])
        for hd in range(PROJ_COLS // HEAD_DIM):
            sl = slice(hd * HEAD_DIM, (hd + 1) * HEAD_DIM)
            t = y[:, sl]
            ms = jnp.mean(t * t, axis=-1, keepdims=True)
            t = t * lax.rsqrt(ms + NORM_EPS) * gain
            t = (t * cos + pltpu.roll(t, HEAD_DIM - half, 1) * slo
                 + pltpu.roll(t, half, 1) * shi)
            o_ref[:, sl] = t.astype(o_ref.dtype)

    @pl.when(j >= (Q_WIDTH + KV_WIDTH) // PROJ_COLS)
    def _():
        o_ref[...] = y.astype(o_ref.dtype)


def _qkv_projection(stream, w, layer, q_gain, k_gain, *, tm, name):
    assert Q_WIDTH % PROJ_COLS == 0 and KV_WIDTH % PROJ_COLS == 0
    hg, ss = stream
    m, k = hg.shape
    cos, slo, shi = _rope_tables()
    table = pl.BlockSpec((tm, HEAD_DIM), lambda i, j: (i, 0))
    gain = pl.BlockSpec((1, HEAD_DIM), lambda i, j: (0, 0))
    return pl.pallas_call(
        _qkv_kernel,
        grid=(m // tm, QKV_WIDTH // PROJ_COLS),
        in_specs=[_panel_spec(tm, k), _ss_spec(tm), _weight_spec(layer, k, PROJ_COLS),
                  table, table, table, gain, gain],
        out_specs=pl.BlockSpec((tm, PROJ_COLS), lambda i, j: (i, j)),
        out_shape=jax.ShapeDtypeStruct((m, QKV_WIDTH), BF16),
        compiler_params=_params(("parallel", "arbitrary")),
        name=name,
    )(hg, ss, w, cos, slo, shi, q_gain.reshape(1, HEAD_DIM),
      k_gain.reshape(1, HEAD_DIM))


ATTN_KV_CHUNK = 512
ATTN_STRIP_ROWS = 32
META_PAD = 128

_NT = (((1,), (1,)), ((), ()))


def _flash_kernel(q_ref, k_ref, v_ref, km_ref, vm_ref, *rest, tq):
    (o_ref, q_all, s0, s1, p0, p1, a0, a1, m_ref, l_ref, acc_ref) = rest[-11:]
    s_buf, p_buf, a_buf = (s0, s1), (p0, p1), (a0, a1)
    nq = GROUP * tq
    strip = min(ATTN_STRIP_ROWS, tq)
    n_chunks = SEQ // ATTN_KV_CHUNK

    for g in range(GROUP):
        q_all[g * tq:(g + 1) * tq, :] = q_ref[:, g * HEAD_DIM:(g + 1) * HEAD_DIM]
    m_ref[...] = jnp.full(m_ref.shape, -jnp.inf, F32)
    l_ref[...] = jnp.zeros(l_ref.shape, F32)

    def scores(kc, buf, width):
        s_buf[buf][:, :width] = lax.dot_general(q_all[...], kc, _NT,
                                                preferred_element_type=F32)

    def softmax(buf, width, n_valid):
        for r in range(nq // strip):
            rs = slice(r * strip, (r + 1) * strip)
            s = s_buf[buf][rs, :width]
            if n_valid < width:
                col = lax.broadcasted_iota(jnp.int32, s.shape, 1)
                s = jnp.where(col < n_valid, s, -jnp.inf)
            m_old = m_ref[rs, :]
            m_new = jnp.maximum(m_old, jnp.max(s, axis=-1, keepdims=True))
            alpha = jnp.exp2(m_old - m_new)
            l_new = alpha * l_ref[rs, :]
            for j in range(width // LANES):
                cols = slice(j * LANES, (j + 1) * LANES)
                p = jnp.exp2(s[:, cols] - m_new)
                l_new = l_new + p
                p_buf[buf][rs, cols] = p.astype(BF16)
            l_ref[rs, :] = l_new
            m_ref[rs, :] = m_new
            a_buf[buf][rs, :] = alpha

    def values(vc, buf, width):
        pv = jnp.dot(p_buf[buf][:, :width], vc, preferred_element_type=F32)
        acc_ref[...] = a_buf[buf][...] * acc_ref[...] + pv

    def k_chunk(c):
        return k_ref[pl.ds(pl.multiple_of(c * ATTN_KV_CHUNK, ATTN_KV_CHUNK),
                           ATTN_KV_CHUNK), :]

    def v_chunk(c):
        return v_ref[pl.ds(pl.multiple_of(c * ATTN_KV_CHUNK, ATTN_KV_CHUNK),
                           ATTN_KV_CHUNK), :]

    pad = jnp.zeros((META_PAD - N_META, HEAD_DIM), BF16)
    scores(jnp.concatenate([km_ref[...], pad], axis=0), 1, META_PAD)
    scores(k_chunk(0), 0, ATTN_KV_CHUNK)
    softmax(1, META_PAD, N_META)
    acc_ref[...] = jnp.dot(p_buf[1][:, :META_PAD],
                           jnp.concatenate([vm_ref[...], pad], axis=0),
                           preferred_element_type=F32)
    scores(k_chunk(1), 1, ATTN_KV_CHUNK)
    softmax(0, ATTN_KV_CHUNK, ATTN_KV_CHUNK)

    def body(i, carry):
        c = 2 * i + 1
        values(v_chunk(c - 1), 0, ATTN_KV_CHUNK)
        scores(k_chunk(c + 1), 0, ATTN_KV_CHUNK)
        softmax(1, ATTN_KV_CHUNK, ATTN_KV_CHUNK)
        values(v_chunk(c), 1, ATTN_KV_CHUNK)
        scores(k_chunk(c + 2), 1, ATTN_KV_CHUNK)
        softmax(0, ATTN_KV_CHUNK, ATTN_KV_CHUNK)
        return carry

    lax.fori_loop(0, (n_chunks - 2) // 2, body, 0)
    values(v_chunk(n_chunks - 2), 0, ATTN_KV_CHUNK)
    softmax(1, ATTN_KV_CHUNK, ATTN_KV_CHUNK)
    values(v_chunk(n_chunks - 1), 1, ATTN_KV_CHUNK)

    for g in range(GROUP):
        rows = slice(g * tq, (g + 1) * tq)
        denom = jnp.sum(l_ref[rows, :], axis=-1, keepdims=True)
        o_ref[:, g * HEAD_DIM:(g + 1) * HEAD_DIM] = (
            acc_ref[rows, :] / denom).astype(o_ref.dtype)


def _flash_scratch(tq):
    nq = GROUP * tq
    scores = pltpu.VMEM((nq, ATTN_KV_CHUNK), F32)
    probs = pltpu.VMEM((nq, ATTN_KV_CHUNK), BF16)
    stat = pltpu.VMEM((nq, LANES), F32)
    return [pltpu.VMEM((nq, HEAD_DIM), BF16),
            scores, scores, probs, probs,
            stat, stat,
            stat,
            stat,
            pltpu.VMEM((nq, HEAD_DIM), F32)]


def _attention(qkv, *, tq):
    qw = GROUP * HEAD_DIM
    n_q = SEQ // tq
    meta0 = M_REAL // N_META
    k0 = Q_WIDTH // HEAD_DIM
    v0 = (Q_WIDTH + KV_WIDTH) // HEAD_DIM

    def kv_real(col0):
        return pl.BlockSpec((SEQ, HEAD_DIM), lambda b, g, i: (b, col0 + g))

    def kv_meta(col0):
        return pl.BlockSpec((N_META, HEAD_DIM), lambda b, g, i: (meta0 + b, col0 + g))

    kv_specs = [kv_real(k0), kv_real(v0), kv_meta(k0), kv_meta(v0)]
    q_spec = pl.BlockSpec((tq, qw), lambda b, g, i: (b * n_q + i, g))
    o_real = pl.pallas_call(
        functools.partial(_flash_kernel, tq=tq),
        grid=(BATCH, N_KV_HEADS, n_q),
        in_specs=[q_spec] + kv_specs,
        out_specs=q_spec,
        out_shape=jax.ShapeDtypeStruct((M_ROWS, Q_WIDTH), BF16),
        scratch_shapes=_flash_scratch(tq),
        compiler_params=_params(("parallel", "parallel", "arbitrary")),
        name="attn_real",
    )(qkv, qkv, qkv, qkv, qkv)

    qm_spec = pl.BlockSpec((N_META, qw), lambda b, g, i: (meta0 + b, g))
    return pl.pallas_call(
        functools.partial(_flash_kernel, tq=N_META),
        grid=(BATCH, N_KV_HEADS, 1),
        in_specs=[qm_spec] + kv_specs + [pl.BlockSpec(memory_space=pl.ANY)],
        out_specs=qm_spec,
        out_shape=jax.ShapeDtypeStruct((M_ROWS, Q_WIDTH), BF16),
        input_output_aliases={5: 0},
        scratch_shapes=_flash_scratch(N_META),
        compiler_params=_params(("parallel", "parallel", "arbitrary")),
        name="attn_meta",
    )(qkv, qkv, qkv, qkv, qkv, o_real)


def _attn_mixer(h, stream, w_qkv, q_gain, k_gain, w_o, layer, next_gain):
    qkv = _qkv_projection(stream, w_qkv, layer, q_gain[layer], k_gain[layer],
                          tm=ROWS_LARGE, name="attn_qkv")
    o = _attention(qkv, tq=ATTN_Q_ROWS)
    return _residual_matmul(o, w_o, layer, h, tm=ROWS_LARGE, tn=PROJ_COLS,
                            scale=1.0, next_gain=next_gain, name="attn_out")


def kernel(x, meta_tokens, ffn_a_norm, ffn_a_w_gate, ffn_a_w_up, ffn_a_w_down, ffn_b_norm, ffn_b_w_gate, ffn_b_w_up, ffn_b_w_down, conv_norm, conv_w_in, conv_w, conv_b, conv_w_out, attn_norm, attn_w_qkv, attn_q_norm, attn_k_norm, attn_w_o, final_norm):
    bsz, n_real, d = x.shape
    assert (bsz, n_real, d) == (BATCH, SEQ, D_MODEL)
    meta = meta_tokens.astype(x.dtype)
    h = jnp.concatenate([x.reshape(M_REAL, D_MODEL)] + [meta] * BATCH, axis=0)
    stream = _prenorm(h, ffn_a_norm[0], block_rows=ROWS_SMALL, name="embed_norm")

    for i in range(DEPTH):
        j = i // 2
        is_conv = i % 2 == 0
        mixer_gain = conv_norm[j] if is_conv else attn_norm[j]
        h, stream = _ffn(h, stream, ffn_a_w_gate, ffn_a_w_up, ffn_a_w_down, i,
                         mixer_gain, name=f"ffn_a{i}")
        if is_conv:
            h, stream = _conv_mixer(h, stream, conv_w_in, conv_w, conv_b,
                                    conv_w_out, j, ffn_b_norm[i])
        else:
            h, stream = _attn_mixer(h, stream, attn_w_qkv, attn_q_norm,
                                    attn_k_norm, attn_w_o, j, ffn_b_norm[i])
        after = ffn_a_norm[i + 1] if i + 1 < DEPTH else None
        h, stream = _ffn(h, stream, ffn_b_w_gate, ffn_b_w_up, ffn_b_w_down, i,
                         after, name=f"ffn_b{i}")

    out = _rmsnorm(h, final_norm, rows=M_REAL, out_dtype=F32, block_rows=FINAL_ROWS,
                   name="final_norm")
    return out.reshape(BATCH, SEQ, D_MODEL)
```

```python
import functools

import jax
import jax.numpy as jnp
from jax import lax
from jax.experimental import pallas as pl
from jax.experimental.pallas import tpu as pltpu

D_MODEL = 4096
BATCH = 2
SEQ = 8192
DEPTH = 2
N_META = 16
GRID_W = 64
D_FF = 11008
N_HEADS = 32
N_KV_HEADS = 8
HEAD_DIM = 128
GROUP = N_HEADS // N_KV_HEADS
ROPE_AXIS_DIM = HEAD_DIM // 2
ROPE_THETA = 10000.0
NORM_EPS = 1e-6
LOG2_E = 1.4426950408889634
MACARON_WEIGHT = 0.5

M_REAL = BATCH * SEQ
M_META = BATCH * N_META
M_ROWS = M_REAL + M_META
Q_WIDTH = N_HEADS * HEAD_DIM
KV_WIDTH = N_KV_HEADS * HEAD_DIM
QKV_WIDTH = Q_WIDTH + 2 * KV_WIDTH

VMEM_LIMIT_BYTES = 56 * 1024 * 1024
SUBLANES = 8
LANES = 128

ROWS_XLARGE = 2736
ROWS_LARGE = 1824
ROWS_MEDIUM = 912
ROWS_SMALL = 304
FINAL_ROWS = 256
ATTN_Q_ROWS = 512
FFN_COLS = 256
PROJ_COLS = 512

F32 = jnp.float32
BF16 = jnp.bfloat16


def _params(semantics):
    return pltpu.CompilerParams(dimension_semantics=semantics,
                                vmem_limit_bytes=VMEM_LIMIT_BYTES)


def _lane_partial_sumsq(y):
    sq = y * y
    part = sq[:, :LANES]
    for c in range(1, y.shape[1] // LANES):
        part = part + sq[:, c * LANES:(c + 1) * LANES]
    return part


def _row_scale(ss_ref):
    total = jnp.sum(ss_ref[...], axis=-1, keepdims=True)
    return lax.rsqrt(total * (1.0 / D_MODEL) + NORM_EPS)


def _prenorm_kernel(h_ref, g_ref, hg_ref, ss_ref):
    x = h_ref[...]
    hg_ref[...] = (x * g_ref[...]).astype(hg_ref.dtype)
    ss_ref[...] = _lane_partial_sumsq(x)


def _prenorm(h, gain, *, block_rows, name):
    rows = h.shape[0]
    row_spec = lambda w: pl.BlockSpec((block_rows, w), lambda i: (i, 0))
    return pl.pallas_call(
        _prenorm_kernel,
        grid=(rows // block_rows,),
        in_specs=[row_spec(D_MODEL), pl.BlockSpec((1, D_MODEL), lambda i: (0, 0))],
        out_specs=[row_spec(D_MODEL), row_spec(LANES)],
        out_shape=[jax.ShapeDtypeStruct((rows, D_MODEL), BF16),
                   jax.ShapeDtypeStruct((rows, LANES), F32)],
        compiler_params=_params(("parallel",)),
        name=name,
    )(h, gain.reshape(1, D_MODEL))


def _rmsnorm_kernel(h_ref, g_ref, o_ref):
    x = h_ref[...]
    ms = jnp.mean(x * x, axis=-1, keepdims=True)
    o_ref[...] = (x * lax.rsqrt(ms + NORM_EPS) * g_ref[...]).astype(o_ref.dtype)


def _rmsnorm(h, gain, *, rows, out_dtype, block_rows, name):
    return pl.pallas_call(
        _rmsnorm_kernel,
        grid=(rows // block_rows,),
        in_specs=[pl.BlockSpec((block_rows, D_MODEL), lambda i: (i, 0)),
                  pl.BlockSpec((1, D_MODEL), lambda i: (0, 0))],
        out_specs=pl.BlockSpec((block_rows, D_MODEL), lambda i: (i, 0)),
        out_shape=jax.ShapeDtypeStruct((rows, D_MODEL), out_dtype),
        compiler_params=_params(("parallel",)),
        name=name,
    )(h, gain.reshape(1, D_MODEL))


def _panel_spec(tm, k):
    return pl.BlockSpec((tm, k), lambda i, j: (i, 0), pipeline_mode=pl.Buffered(1))


def _weight_spec(layer, k, tn):
    return pl.BlockSpec((None, k, tn), lambda i, j: (layer, 0, j))


def _ss_spec(tm):
    return pl.BlockSpec((tm, LANES), lambda i, j: (i, 0))


def _normed_mm_kernel(x_ref, ss_ref, w_ref, o_ref):
    y = jnp.dot(x_ref[...], w_ref[...].astype(BF16), preferred_element_type=F32)
    o_ref[...] = (_row_scale(ss_ref) * y).astype(o_ref.dtype)


def _normed_matmul(stream, w, layer, *, tm, tn, out_dtype, name):
    hg, ss = stream
    m, k = hg.shape
    n = w.shape[2]
    return pl.pallas_call(
        _normed_mm_kernel,
        grid=(m // tm, n // tn),
        in_specs=[_panel_spec(tm, k), _ss_spec(tm), _weight_spec(layer, k, tn)],
        out_specs=pl.BlockSpec((tm, tn), lambda i, j: (i, j)),
        out_shape=jax.ShapeDtypeStruct((m, n), out_dtype),
        compiler_params=_params(("parallel", "arbitrary")),
        name=name,
    )(hg, ss, w)


def _gate_up_kernel(x_ref, ss_ref, wg_ref, wu_ref, o_ref):
    x = x_ref[...]
    r = _row_scale(ss_ref)
    g = r * jnp.dot(x, wg_ref[...].astype(BF16), preferred_element_type=F32)
    u = r * jnp.dot(x, wu_ref[...].astype(BF16), preferred_element_type=F32)
    o_ref[...] = (g * jax.nn.sigmoid(g) * u).astype(o_ref.dtype)


def _gate_up(stream, wg, wu, layer, *, tm, tn, name):
    hg, ss = stream
    m, k = hg.shape
    n = wg.shape[2]
    return pl.pallas_call(
        _gate_up_kernel,
        grid=(m // tm, n // tn),
        in_specs=[_panel_spec(tm, k), _ss_spec(tm),
                  _weight_spec(layer, k, tn), _weight_spec(layer, k, tn)],
        out_specs=pl.BlockSpec((tm, tn), lambda i, j: (i, j)),
        out_shape=jax.ShapeDtypeStruct((m, n), BF16),
        compiler_params=_params(("parallel", "arbitrary")),
        name=name,
    )(hg, ss, wg, wu)


def _residual_kernel(x_ref, w_ref, h_ref, o_ref, *, scale):
    y = jnp.dot(x_ref[...], w_ref[...].astype(BF16), preferred_element_type=F32)
    o_ref[...] = h_ref[...] + scale * y


def _residual_prenorm_kernel(x_ref, w_ref, h_ref, g_ref, o_ref, hg_ref, ss_ref,
                             *, scale):
    y = jnp.dot(x_ref[...], w_ref[...].astype(BF16), preferred_element_type=F32)
    y = h_ref[...] + scale * y
    o_ref[...] = y
    hg_ref[...] = (y * g_ref[...]).astype(hg_ref.dtype)
    part = _lane_partial_sumsq(y)
    j = pl.program_id(1)

    @pl.when(j == 0)
    def _():
        ss_ref[...] = part

    @pl.when(j > 0)
    def _():
        ss_ref[...] += part


def _residual_matmul(x, w, layer, h, *, tm, tn, scale, name, next_gain=None):
    m, k = x.shape
    n = w.shape[2]
    tile = pl.BlockSpec((tm, tn), lambda i, j: (i, j))
    in_specs = [_panel_spec(tm, k), _weight_spec(layer, k, tn), tile]
    h_shape = jax.ShapeDtypeStruct((m, n), F32)
    if next_gain is None:
        return pl.pallas_call(
            functools.partial(_residual_kernel, scale=scale),
            grid=(m // tm, n // tn),
            in_specs=in_specs,
            out_specs=tile,
            out_shape=h_shape,
            compiler_params=_params(("parallel", "arbitrary")),
            name=name,
        )(x, w, h), None
    h_new, hg, ss = pl.pallas_call(
        functools.partial(_residual_prenorm_kernel, scale=scale),
        grid=(m // tm, n // tn),
        in_specs=in_specs + [pl.BlockSpec((1, tn), lambda i, j: (0, j))],
        out_specs=[tile, tile, _ss_spec(tm)],
        out_shape=[h_shape, jax.ShapeDtypeStruct((m, n), BF16),
                   jax.ShapeDtypeStruct((m, LANES), F32)],
        compiler_params=_params(("parallel", "arbitrary")),
        name=name,
    )(x, w, h, next_gain.reshape(1, n))
    return h_new, (hg, ss)


def _ffn(h, stream, wg, wu, wd, layer, next_gain, *, name):
    act = _gate_up(stream, wg, wu, layer, tm=ROWS_XLARGE, tn=FFN_COLS,
                   name=name + "_gate_up")
    return _residual_matmul(act, wd, layer, h, tm=ROWS_MEDIUM, tn=FFN_COLS,
                            scale=MACARON_WEIGHT, next_gain=next_gain,
                            name=name + "_down")


CONV_ROWS = 512
CONV_COLS = 512
CONV_COL_BLOCKS = D_MODEL // CONV_COLS
CONV_TILES_PER_BATCH = SEQ // CONV_ROWS


def _conv_taps(u, u_left, u_right, b_gate, w_ref, bias_ref, o_ref):
    conv = (u_left * w_ref[0:1, :] + u * w_ref[1:2, :] + u_right * w_ref[2:3, :]
            + bias_ref[...])
    o_ref[...] = (b_gate * conv).astype(o_ref.dtype)


def _conv_real_kernel(b_ref, c_ref, h_ref, cp_ref, hp_ref, cn_ref, hn_ref,
                      w_ref, bias_ref, o_ref):
    rows = c_ref.shape[0]
    u = c_ref[...] * h_ref[...]
    u_prev = cp_ref[SUBLANES - 1:SUBLANES, :] * hp_ref[SUBLANES - 1:SUBLANES, :]
    last_in_batch = (pl.program_id(0) % CONV_TILES_PER_BATCH
                     == CONV_TILES_PER_BATCH - 1)
    u_next = jnp.where(last_in_batch, 0.0, cn_ref[0:1, :] * hn_ref[0:1, :])
    row = lax.broadcasted_iota(jnp.int32, u.shape, 0)
    u_left = jnp.where(row == 0, u_prev, pltpu.roll(u, 1, 0))
    u_right = jnp.where(row == rows - 1, u_next, pltpu.roll(u, rows - 1, 0))
    _conv_taps(u, u_left, u_right, b_ref[...], w_ref, bias_ref, o_ref)


def _conv_meta_kernel(b_ref, c_ref, h_ref, cn_ref, hn_ref, w_ref, bias_ref,
                      y_hbm_ref, o_ref):
    del y_hbm_ref
    u = c_ref[...] * h_ref[...]
    u_next = cn_ref[0:1, :] * hn_ref[0:1, :]
    row = lax.broadcasted_iota(jnp.int32, u.shape, 0)
    u_left = jnp.where(row == 0, 0.0, pltpu.roll(u, 1, 0))
    u_right = jnp.where(row == N_META - 1, u_next, pltpu.roll(u, N_META - 1, 0))
    _conv_taps(u, u_left, u_right, b_ref[...], w_ref, bias_ref, o_ref)


def _conv_gate(z, conv_w, conv_b):
    nb = CONV_COL_BLOCKS
    halo_per_tile = CONV_ROWS // SUBLANES
    meta_halo0 = (M_REAL + N_META - SUBLANES) // SUBLANES
    meta_halo_step = N_META // SUBLANES

    def prev_block(i):
        first = i % CONV_TILES_PER_BATCH == 0
        batch = i // CONV_TILES_PER_BATCH
        return jnp.where(first, meta_halo0 + meta_halo_step * batch,
                         i * halo_per_tile - 1)

    def next_block(i):
        last = i % CONV_TILES_PER_BATCH == CONV_TILES_PER_BATCH - 1
        return jnp.where(last, i * halo_per_tile, (i + 1) * halo_per_tile)

    def tile(col0):
        return pl.BlockSpec((CONV_ROWS, CONV_COLS), lambda i, j: (i, j + col0))

    def halo(block_fn, col0):
        return pl.BlockSpec((SUBLANES, CONV_COLS),
                            lambda i, j: (block_fn(i), j + col0))

    w_spec = pl.BlockSpec((3, CONV_COLS), lambda i, j: (0, j))
    bias_spec = pl.BlockSpec((1, CONV_COLS), lambda i, j: (0, j))
    bias = conv_b.reshape(1, D_MODEL)
    y = pl.pallas_call(
        _conv_real_kernel,
        grid=(M_REAL // CONV_ROWS, nb),
        in_specs=[tile(0), tile(nb), tile(2 * nb),
                  halo(prev_block, nb), halo(prev_block, 2 * nb),
                  halo(next_block, nb), halo(next_block, 2 * nb),
                  w_spec, bias_spec],
        out_specs=pl.BlockSpec((CONV_ROWS, CONV_COLS), lambda i, j: (i, j)),
        out_shape=jax.ShapeDtypeStruct((M_ROWS, D_MODEL), BF16),
        compiler_params=_params(("parallel", "parallel")),
        name="conv_real",
    )(z, z, z, z, z, z, z, conv_w, bias)

    meta0 = M_REAL // N_META
    real0_step = SEQ // SUBLANES

    def meta_tile(col0):
        return pl.BlockSpec((N_META, CONV_COLS), lambda b, j: (meta0 + b, j + col0))

    def first_real(col0):
        return pl.BlockSpec((SUBLANES, CONV_COLS),
                            lambda b, j: (b * real0_step, j + col0))

    w_spec = pl.BlockSpec((3, CONV_COLS), lambda b, j: (0, j))
    bias_spec = pl.BlockSpec((1, CONV_COLS), lambda b, j: (0, j))
    return pl.pallas_call(
        _conv_meta_kernel,
        grid=(BATCH, nb),
        in_specs=[meta_tile(0), meta_tile(nb), meta_tile(2 * nb),
                  first_real(nb), first_real(2 * nb), w_spec, bias_spec,
                  pl.BlockSpec(memory_space=pl.ANY)],
        out_specs=pl.BlockSpec((N_META, CONV_COLS), lambda b, j: (meta0 + b, j)),
        out_shape=jax.ShapeDtypeStruct((M_ROWS, D_MODEL), BF16),
        input_output_aliases={7: 0},
        compiler_params=_params(("parallel", "parallel")),
        name="conv_meta",
    )(z, z, z, z, z, conv_w, bias, y)


def _conv_mixer(h, stream, w_in, conv_w, conv_b, w_out, layer, next_gain):
    z = _normed_matmul(stream, w_in, layer, tm=ROWS_LARGE, tn=PROJ_COLS,
                       out_dtype=F32, name="conv_in")
    y = _conv_gate(z, conv_w[layer], conv_b[layer])
    return _residual_matmul(y, w_out, layer, h, tm=ROWS_LARGE, tn=PROJ_COLS,
                            scale=1.0, next_gain=next_gain, name="conv_out")


def _rope_tables():
    t = jnp.arange(SEQ, dtype=jnp.int32)
    real_row = (t // GRID_W).astype(F32)
    real_col = (t % GRID_W).astype(F32)
    meta_row = jnp.full((N_META,), -1.0, F32)
    meta_col = jnp.arange(N_META, dtype=F32)
    row = jnp.concatenate([jnp.tile(real_row, BATCH), jnp.tile(meta_row, BATCH)])
    col = jnp.concatenate([jnp.tile(real_col, BATCH), jnp.tile(meta_col, BATCH)])
    inv_freq = ROPE_THETA ** (-jnp.arange(0, ROPE_AXIS_DIM, 2, dtype=F32)
                              / ROPE_AXIS_DIM)
    ar = row[:, None] * inv_freq
    ac = col[:, None] * inv_freq
    zero = jnp.zeros_like(ar)
    cos = jnp.concatenate([jnp.cos(ar), jnp.cos(ar), jnp.cos(ac), jnp.cos(ac)], -1)
    sin_lo = jnp.concatenate([-jnp.sin(ar), zero, -jnp.sin(ac), zero], -1)
    sin_hi = jnp.concatenate([zero, jnp.sin(ar), zero, jnp.sin(ac)], -1)
    return cos, sin_lo, sin_hi


def _qkv_kernel(x_ref, ss_ref, w_ref, cos_ref, slo_ref, shi_ref, qg_ref, kg_ref,
                o_ref):
    j = pl.program_id(1)
    y = _row_scale(ss_ref) * jnp.dot(x_ref[...], w_ref[...].astype(BF16),
                                     preferred_element_type=F32)
    half = ROPE_AXIS_DIM // 2

    @pl.when(j < (Q_WIDTH + KV_WIDTH) // PROJ_COLS)
    def _():
        cos = cos_ref[...]
        slo = slo_ref[...]
        shi = shi_ref[...]
        gain = jnp.where(j < Q_WIDTH // PROJ_COLS,
                         qg_ref[...] * (HEAD_DIM ** -0.5 * LOG2_E), kg_ref[...])
        for hd in range(PROJ_COLS // HEAD_DIM):
            sl = slice(hd * HEAD_DIM, (hd + 1) * HEAD_DIM)
            t = y[:, sl]
            ms = jnp.mean(t * t, axis=-1, keepdims=True)
            t = t * lax.rsqrt(ms + NORM_EPS) * gain
            t = (t * cos + pltpu.roll(t, HEAD_DIM - half, 1) * slo
                 + pltpu.roll(t, half, 1) * shi)
            o_ref[:, sl] = t.astype(o_ref.dtype)

    @pl.when(j >= (Q_WIDTH + KV_WIDTH) // PROJ_COLS)
    def _():
        o_ref[...] = y.astype(o_ref.dtype)


def _qkv_projection(stream, w, layer, q_gain, k_gain, *, tm, name):
    assert Q_WIDTH % PROJ_COLS == 0 and KV_WIDTH % PROJ_COLS == 0
    hg, ss = stream
    m, k = hg.shape
    cos, slo, shi = _rope_tables()
    table = pl.BlockSpec((tm, HEAD_DIM), lambda i, j: (i, 0))
    gain = pl.BlockSpec((1, HEAD_DIM), lambda i, j: (0, 0))
    return pl.pallas_call(
        _qkv_kernel,
        grid=(m // tm, QKV_WIDTH // PROJ_COLS),
        in_specs=[_panel_spec(tm, k), _ss_spec(tm), _weight_spec(layer, k, PROJ_COLS),
                  table, table, table, gain, gain],
        out_specs=pl.BlockSpec((tm, PROJ_COLS), lambda i, j: (i, j)),
        out_shape=jax.ShapeDtypeStruct((m, QKV_WIDTH), BF16),
        compiler_params=_params(("parallel", "arbitrary")),
        name=name,
    )(hg, ss, w, cos, slo, shi, q_gain.reshape(1, HEAD_DIM),
      k_gain.reshape(1, HEAD_DIM))


ATTN_KV_CHUNK = 512
ATTN_STRIP_ROWS = 32
META_PAD = 128

_NT = (((1,), (1,)), ((), ()))


def _flash_kernel(q_ref, k_ref, v_ref, km_ref, vm_ref, *rest, tq):
    (o_ref, q_all, s0, s1, p0, p1, a0, a1, m_ref, l_ref, acc_ref) = rest[-11:]
    s_buf, p_buf, a_buf = (s0, s1), (p0, p1), (a0, a1)
    nq = GROUP * tq
    strip = min(ATTN_STRIP_ROWS, tq)
    n_chunks = SEQ // ATTN_KV_CHUNK

    for g in range(GROUP):
        q_all[g * tq:(g + 1) * tq, :] = q_ref[:, g * HEAD_DIM:(g + 1) * HEAD_DIM]
    m_ref[...] = jnp.full(m_ref.shape, -jnp.inf, F32)
    l_ref[...] = jnp.zeros(l_ref.shape, F32)

    def scores(kc, buf, width):
        s_buf[buf][:, :width] = lax.dot_general(q_all[...], kc, _NT,
                                                preferred_element_type=F32)

    def softmax(buf, width, n_valid):
        for r in range(nq // strip):
            rs = slice(r * strip, (r + 1) * strip)
            s = s_buf[buf][rs, :width]
            if n_valid < width:
                col = lax.broadcasted_iota(jnp.int32, s.shape, 1)
                s = jnp.where(col < n_valid, s, -jnp.inf)
            m_old = m_ref[rs, :]
            m_new = jnp.maximum(m_old, jnp.max(s, axis=-1, keepdims=True))
            alpha = jnp.exp2(m_old - m_new)
            l_new = alpha * l_ref[rs, :]
            for j in range(width // LANES):
                cols = slice(j * LANES, (j + 1) * LANES)
                p = jnp.exp2(s[:, cols] - m_new)
                l_new = l_new + p
                p_buf[buf][rs, cols] = p.astype(BF16)
            l_ref[rs, :] = l_new
            m_ref[rs, :] = m_new
            a_buf[buf][rs, :] = alpha

    def values(vc, buf, width):
        pv = jnp.dot(p_buf[buf][:, :width], vc, preferred_element_type=F32)
        acc_ref[...] = a_buf[buf][...] * acc_ref[...] + pv

    def k_chunk(c):
        return k_ref[pl.ds(pl.multiple_of(c * ATTN_KV_CHUNK, ATTN_KV_CHUNK),
                           ATTN_KV_CHUNK), :]

    def v_chunk(c):
        return v_ref[pl.ds(pl.multiple_of(c * ATTN_KV_CHUNK, ATTN_KV_CHUNK),
                           ATTN_KV_CHUNK), :]

    pad = jnp.zeros((META_PAD - N_META, HEAD_DIM), BF16)
    scores(jnp.concatenate([km_ref[...], pad], axis=0), 1, META_PAD)
    scores(k_chunk(0), 0, ATTN_KV_CHUNK)
    softmax(1, META_PAD, N_META)
    acc_ref[...] = jnp.dot(p_buf[1][:, :META_PAD],
                           jnp.concatenate([vm_ref[...], pad], axis=0),
                           preferred_element_type=F32)
    scores(k_chunk(1), 1, ATTN_KV_CHUNK)
    softmax(0, ATTN_KV_CHUNK, ATTN_KV_CHUNK)

    def body(i, carry):
        c = 2 * i + 1
        values(v_chunk(c - 1), 0, ATTN_KV_CHUNK)
        scores(k_chunk(c + 1), 0, ATTN_KV_CHUNK)
        softmax(1, ATTN_KV_CHUNK, ATTN_KV_CHUNK)
        values(v_chunk(c), 1, ATTN_KV_CHUNK)
        scores(k_chunk(c + 2), 1, ATTN_KV_CHUNK)
        softmax(0, ATTN_KV_CHUNK, ATTN_KV_CHUNK)
        return carry

    lax.fori_loop(0, (n_chunks - 2) // 2, body, 0)
    values(v_chunk(n_chunks - 2), 0, ATTN_KV_CHUNK)
    softmax(1, ATTN_KV_CHUNK, ATTN_KV_CHUNK)
    values(v_chunk(n_chunks - 1), 1, ATTN_KV_CHUNK)

    for g in range(GROUP):
        rows = slice(g * tq, (g + 1) * tq)
        denom = jnp.sum(l_ref[rows, :], axis=-1, keepdims=True)
        o_ref[:, g * HEAD_DIM:(g + 1) * HEAD_DIM] = (
            acc_ref[rows, :] / denom).astype(o_ref.dtype)


def _flash_scratch(tq):
    nq = GROUP * tq
    scores = pltpu.VMEM((nq, ATTN_KV_CHUNK), F32)
    probs = pltpu.VMEM((nq, ATTN_KV_CHUNK), BF16)
    stat = pltpu.VMEM((nq, LANES), F32)
    return [pltpu.VMEM((nq, HEAD_DIM), BF16),
            scores, scores, probs, probs,
            stat, stat,
            stat,
            stat,
            pltpu.VMEM((nq, HEAD_DIM), F32)]


def _attention(qkv, *, tq):
    qw = GROUP * HEAD_DIM
    n_q = SEQ // tq
    meta0 = M_REAL // N_META
    k0 = Q_WIDTH // HEAD_DIM
    v0 = (Q_WIDTH + KV_WIDTH) // HEAD_DIM

    def kv_real(col0):
        return pl.BlockSpec((SEQ, HEAD_DIM), lambda b, g, i: (b, col0 + g))

    def kv_meta(col0):
        return pl.BlockSpec((N_META, HEAD_DIM), lambda b, g, i: (meta0 + b, col0 + g))

    kv_specs = [kv_real(k0), kv_real(v0), kv_meta(k0), kv_meta(v0)]
    q_spec = pl.BlockSpec((tq, qw), lambda b, g, i: (b * n_q + i, g))
    o_real = pl.pallas_call(
        functools.partial(_flash_kernel, tq=tq),
        grid=(BATCH, N_KV_HEADS, n_q),
        in_specs=[q_spec] + kv_specs,
        out_specs=q_spec,
        out_shape=jax.ShapeDtypeStruct((M_ROWS, Q_WIDTH), BF16),
        scratch_shapes=_flash_scratch(tq),
        compiler_params=_params(("parallel", "parallel", "arbitrary")),
        name="attn_real",
    )(qkv, qkv, qkv, qkv, qkv)

    qm_spec = pl.BlockSpec((N_META, qw), lambda b, g, i: (meta0 + b, g))
    return pl.pallas_call(
        functools.partial(_flash_kernel, tq=N_META),
        grid=(BATCH, N_KV_HEADS, 1),
        in_specs=[qm_spec] + kv_specs + [pl.BlockSpec(memory_space=pl.ANY)],
        out_specs=qm_spec,
        out_shape=jax.ShapeDtypeStruct((M_ROWS, Q_WIDTH), BF16),
        input_output_aliases={5: 0},
        scratch_shapes=_flash_scratch(N_META),
        compiler_params=_params(("parallel", "parallel", "arbitrary")),
        name="attn_meta",
    )(qkv, qkv, qkv, qkv, qkv, o_real)


def _attn_mixer(h, stream, w_qkv, q_gain, k_gain, w_o, layer, next_gain):
    qkv = _qkv_projection(stream, w_qkv, layer, q_gain[layer], k_gain[layer],
                          tm=ROWS_LARGE, name="attn_qkv")
    o = _attention(qkv, tq=ATTN_Q_ROWS)
    return _residual_matmul(o, w_o, layer, h, tm=ROWS_LARGE, tn=PROJ_COLS,
                            scale=1.0, next_gain=next_gain, name="attn_out")


def kernel(x, meta_tokens, ffn_a_norm, ffn_a_w_gate, ffn_a_w_up, ffn_a_w_down, ffn_b_norm, ffn_b_w_gate, ffn_b_w_up, ffn_b_w_down, conv_norm, conv_w_in, conv_w, conv_b, conv_w_out, attn_norm, attn_w_qkv, attn_q_norm, attn_k_norm, attn_w_o, final_norm):
    bsz, n_real, d = x.shape
    assert (bsz, n_real, d) == (BATCH, SEQ, D_MODEL)
    meta = meta_tokens.astype(x.dtype)
    h = jnp.concatenate([x.reshape(M_REAL, D_MODEL)] + [meta] * BATCH, axis=0)
    stream = _prenorm(h, ffn_a_norm[0], block_rows=ROWS_SMALL, name="embed_norm")

    for i in range(DEPTH):
        j = i // 2
        is_conv = i % 2 == 0
        mixer_gain = conv_norm[j] if is_conv else attn_norm[j]
        h, stream = _ffn(h, stream, ffn_a_w_gate, ffn_a_w_up, ffn_a_w_down, i,
                         mixer_gain, name=f"ffn_a{i}")
        if is_conv:
            h, stream = _conv_mixer(h, stream, conv_w_in, conv_w, conv_b,
                                    conv_w_out, j, ffn_b_norm[i])
        else:
            h, stream = _attn_mixer(h, stream, attn_w_qkv, attn_q_norm,
                                    attn_k_norm, attn_w_o, j, ffn_b_norm[i])
        after = ffn_a_norm[i + 1] if i + 1 < DEPTH else None
        h, stream = _ffn(h, stream, ffn_b_w_gate, ffn_b_w_up, ffn_b_w_down, i,
                         after, name=f"ffn_b{i}")

    out = _rmsnorm(h, final_norm, rows=M_REAL, out_dtype=F32, block_rows=FINAL_ROWS,
                   name="final_norm")
    return out.reshape(BATCH, SEQ, D_MODEL)
```

```python
import functools

import jax
import jax.numpy as jnp
from jax import lax
from jax.experimental import pallas as pl
from jax.experimental.pallas import tpu as pltpu

D_MODEL = 4096
BATCH = 2
SEQ = 8192
DEPTH = 2
N_META = 16
GRID_W = 64
D_FF = 11008
N_HEADS = 32
N_KV_HEADS = 8
HEAD_DIM = 128
GROUP = N_HEADS // N_KV_HEADS
ROPE_AXIS_DIM = HEAD_DIM // 2
ROPE_THETA = 10000.0
NORM_EPS = 1e-6
LOG2_E = 1.4426950408889634
MACARON_WEIGHT = 0.5

M_REAL = BATCH * SEQ
M_META = BATCH * N_META
M_ROWS = M_REAL + M_META
Q_WIDTH = N_HEADS * HEAD_DIM
KV_WIDTH = N_KV_HEADS * HEAD_DIM
QKV_WIDTH = Q_WIDTH + 2 * KV_WIDTH

VMEM_LIMIT_BYTES = 56 * 1024 * 1024
SUBLANES = 8
LANES = 128

ROWS_XLARGE = 2736
ROWS_LARGE = 1824
ROWS_MEDIUM = 912
REAL_ROWS = 256
ATTN_Q_ROWS = 1024
FFN_COLS = 256
PROJ_COLS = 512

F32 = jnp.float32
BF16 = jnp.bfloat16


def _params(semantics):
    return pltpu.CompilerParams(dimension_semantics=semantics,
                                vmem_limit_bytes=VMEM_LIMIT_BYTES)


def _lane_partial_sumsq(y):
    sq = y * y
    part = sq[:, :LANES]
    for c in range(1, y.shape[1] // LANES):
        part = part + sq[:, c * LANES:(c + 1) * LANES]
    return part


def _row_scale(ss_ref):
    total = jnp.sum(ss_ref[...], axis=-1, keepdims=True)
    return lax.rsqrt(total * (1.0 / D_MODEL) + NORM_EPS)


def _embed_rows(x, g_ref, h_ref, hg_ref, ss_ref):
    h_ref[...] = x
    hg_ref[...] = (x * g_ref[...]).astype(hg_ref.dtype)
    ss_ref[...] = _lane_partial_sumsq(x)


def _embed_real_kernel(x_ref, g_ref, h_ref, hg_ref, ss_ref):
    _embed_rows(x_ref[...], g_ref, h_ref, hg_ref, ss_ref)


def _embed_meta_kernel(meta_ref, g_ref, h_hbm, hg_hbm, ss_hbm, h_ref, hg_ref, ss_ref):
    del h_hbm, hg_hbm, ss_hbm
    meta = meta_ref[...]
    _embed_rows(jnp.concatenate([meta] * BATCH, axis=0), g_ref, h_ref, hg_ref, ss_ref)


def _embed(x, meta, gain, *, block_rows):
    row_spec = lambda r, w, i0: pl.BlockSpec((r, w), lambda i: (i + i0, 0))
    gain_spec = pl.BlockSpec((1, D_MODEL), lambda i: (0, 0))
    shapes = [jax.ShapeDtypeStruct((M_ROWS, D_MODEL), F32),
              jax.ShapeDtypeStruct((M_ROWS, D_MODEL), BF16),
              jax.ShapeDtypeStruct((M_ROWS, LANES), F32)]
    gain = gain.reshape(1, D_MODEL)
    outs = pl.pallas_call(
        _embed_real_kernel,
        grid=(M_REAL // block_rows,),
        in_specs=[row_spec(block_rows, D_MODEL, 0), gain_spec],
        out_specs=[row_spec(block_rows, D_MODEL, 0), row_spec(block_rows, D_MODEL, 0),
                   row_spec(block_rows, LANES, 0)],
        out_shape=shapes,
        compiler_params=_params(("parallel",)),
        name="embed_real",
    )(x.reshape(M_REAL, D_MODEL), gain)
    meta0 = M_REAL // M_META
    any_spec = pl.BlockSpec(memory_space=pl.ANY)
    h, hg, ss = pl.pallas_call(
        _embed_meta_kernel,
        grid=(1,),
        in_specs=[pl.BlockSpec((N_META, D_MODEL), lambda i: (0, 0)), gain_spec,
                  any_spec, any_spec, any_spec],
        out_specs=[row_spec(M_META, D_MODEL, meta0), row_spec(M_META, D_MODEL, meta0),
                   row_spec(M_META, LANES, meta0)],
        out_shape=shapes,
        input_output_aliases={2: 0, 3: 1, 4: 2},
        compiler_params=_params(("arbitrary",)),
        name="embed_meta",
    )(meta, gain, *outs)
    return h, (hg, ss)


def _rmsnorm_kernel(h_ref, g_ref, o_ref):
    x = h_ref[...]
    ms = jnp.mean(x * x, axis=-1, keepdims=True)
    o_ref[...] = (x * lax.rsqrt(ms + NORM_EPS) * g_ref[...]).astype(o_ref.dtype)


def _rmsnorm(h, gain, *, rows, out_dtype, block_rows, name):
    return pl.pallas_call(
        _rmsnorm_kernel,
        grid=(rows // block_rows,),
        in_specs=[pl.BlockSpec((block_rows, D_MODEL), lambda i: (i, 0)),
                  pl.BlockSpec((1, D_MODEL), lambda i: (0, 0))],
        out_specs=pl.BlockSpec((block_rows, D_MODEL), lambda i: (i, 0)),
        out_shape=jax.ShapeDtypeStruct((rows, D_MODEL), out_dtype),
        compiler_params=_params(("parallel",)),
        name=name,
    )(h, gain.reshape(1, D_MODEL))


def _panel_spec(tm, k):
    return pl.BlockSpec((tm, k), lambda i, j: (i, 0), pipeline_mode=pl.Buffered(1))


def _weight_spec(layer, k, tn):
    return pl.BlockSpec((None, k, tn), lambda i, j: (layer, 0, j))


def _ss_spec(tm):
    return pl.BlockSpec((tm, LANES), lambda i, j: (i, 0))


def _normed_mm_kernel(x_ref, ss_ref, w_ref, o_ref):
    y = jnp.dot(x_ref[...], w_ref[...].astype(BF16), preferred_element_type=F32)
    o_ref[...] = (_row_scale(ss_ref) * y).astype(o_ref.dtype)


def _normed_matmul(stream, w, layer, *, tm, tn, out_dtype, name):
    hg, ss = stream
    m, k = hg.shape
    n = w.shape[2]
    return pl.pallas_call(
        _normed_mm_kernel,
        grid=(m // tm, n // tn),
        in_specs=[_panel_spec(tm, k), _ss_spec(tm), _weight_spec(layer, k, tn)],
        out_specs=pl.BlockSpec((tm, tn), lambda i, j: (i, j)),
        out_shape=jax.ShapeDtypeStruct((m, n), out_dtype),
        compiler_params=_params(("parallel", "arbitrary")),
        name=name,
    )(hg, ss, w)


def _gate_up_kernel(x_ref, ss_ref, wg_ref, wu_ref, o_ref):
    x = x_ref[...]
    r = _row_scale(ss_ref)
    g = r * jnp.dot(x, wg_ref[...].astype(BF16), preferred_element_type=F32)
    u = r * jnp.dot(x, wu_ref[...].astype(BF16), preferred_element_type=F32)
    o_ref[...] = (g * jax.nn.sigmoid(g) * u).astype(o_ref.dtype)


def _gate_up(stream, wg, wu, layer, *, tm, tn, name):
    hg, ss = stream
    m, k = hg.shape
    n = wg.shape[2]
    return pl.pallas_call(
        _gate_up_kernel,
        grid=(m // tm, n // tn),
        in_specs=[_panel_spec(tm, k), _ss_spec(tm),
                  _weight_spec(layer, k, tn), _weight_spec(layer, k, tn)],
        out_specs=pl.BlockSpec((tm, tn), lambda i, j: (i, j)),
        out_shape=jax.ShapeDtypeStruct((m, n), BF16),
        compiler_params=_params(("parallel", "arbitrary")),
        name=name,
    )(hg, ss, wg, wu)


def _residual_kernel(x_ref, w_ref, h_ref, o_ref, *, scale):
    y = jnp.dot(x_ref[...], w_ref[...].astype(BF16), preferred_element_type=F32)
    o_ref[...] = h_ref[...] + scale * y


def _residual_prenorm_kernel(x_ref, w_ref, h_ref, g_ref, o_ref, hg_ref, ss_ref,
                             *, scale):
    y = jnp.dot(x_ref[...], w_ref[...].astype(BF16), preferred_element_type=F32)
    y = h_ref[...] + scale * y
    o_ref[...] = y
    hg_ref[...] = (y * g_ref[...]).astype(hg_ref.dtype)
    part = _lane_partial_sumsq(y)
    j = pl.program_id(1)

    @pl.when(j == 0)
    def _():
        ss_ref[...] = part

    @pl.when(j > 0)
    def _():
        ss_ref[...] += part


def _residual_matmul(x, w, layer, h, *, tm, tn, scale, name, next_gain=None):
    m, k = x.shape
    n = w.shape[2]
    tile = pl.BlockSpec((tm, tn), lambda i, j: (i, j))
    in_specs = [_panel_spec(tm, k), _weight_spec(layer, k, tn), tile]
    h_shape = jax.ShapeDtypeStruct((m, n), F32)
    if next_gain is None:
        return pl.pallas_call(
            functools.partial(_residual_kernel, scale=scale),
            grid=(m // tm, n // tn),
            in_specs=in_specs,
            out_specs=tile,
            out_shape=h_shape,
            compiler_params=_params(("parallel", "arbitrary")),
            name=name,
        )(x, w, h), None
    h_new, hg, ss = pl.pallas_call(
        functools.partial(_residual_prenorm_kernel, scale=scale),
        grid=(m // tm, n // tn),
        in_specs=in_specs + [pl.BlockSpec((1, tn), lambda i, j: (0, j))],
        out_specs=[tile, tile, _ss_spec(tm)],
        out_shape=[h_shape, jax.ShapeDtypeStruct((m, n), BF16),
                   jax.ShapeDtypeStruct((m, LANES), F32)],
        compiler_params=_params(("parallel", "arbitrary")),
        name=name,
    )(x, w, h, next_gain.reshape(1, n))
    return h_new, (hg, ss)


def _ffn(h, stream, wg, wu, wd, layer, next_gain, *, name):
    act = _gate_up(stream, wg, wu, layer, tm=ROWS_XLARGE, tn=FFN_COLS,
                   name=name + "_gate_up")
    return _residual_matmul(act, wd, layer, h, tm=ROWS_MEDIUM, tn=FFN_COLS,
                            scale=MACARON_WEIGHT, next_gain=next_gain,
                            name=name + "_down")


CONV_ROWS = 512
CONV_COLS = 512
CONV_COL_BLOCKS = D_MODEL // CONV_COLS
CONV_TILES_PER_BATCH = SEQ // CONV_ROWS


def _conv_taps(u, u_left, u_right, b_gate, w_ref, bias_ref, o_ref):
    conv = (u_left * w_ref[0:1, :] + u * w_ref[1:2, :] + u_right * w_ref[2:3, :]
            + bias_ref[...])
    o_ref[...] = (b_gate * conv).astype(o_ref.dtype)


def _conv_real_kernel(b_ref, c_ref, h_ref, cp_ref, hp_ref, cn_ref, hn_ref,
                      w_ref, bias_ref, o_ref):
    rows = c_ref.shape[0]
    u = c_ref[...] * h_ref[...]
    u_prev = cp_ref[SUBLANES - 1:SUBLANES, :] * hp_ref[SUBLANES - 1:SUBLANES, :]
    last_in_batch = (pl.program_id(0) % CONV_TILES_PER_BATCH
                     == CONV_TILES_PER_BATCH - 1)
    u_next = jnp.where(last_in_batch, 0.0, cn_ref[0:1, :] * hn_ref[0:1, :])
    row = lax.broadcasted_iota(jnp.int32, u.shape, 0)
    u_left = jnp.where(row == 0, u_prev, pltpu.roll(u, 1, 0))
    u_right = jnp.where(row == rows - 1, u_next, pltpu.roll(u, rows - 1, 0))
    _conv_taps(u, u_left, u_right, b_ref[...], w_ref, bias_ref, o_ref)


def _conv_meta_kernel(b_ref, c_ref, h_ref, cn_ref, hn_ref, w_ref, bias_ref,
                      y_hbm_ref, o_ref):
    del y_hbm_ref
    u = c_ref[...] * h_ref[...]
    u_next = cn_ref[0:1, :] * hn_ref[0:1, :]
    row = lax.broadcasted_iota(jnp.int32, u.shape, 0)
    u_left = jnp.where(row == 0, 0.0, pltpu.roll(u, 1, 0))
    u_right = jnp.where(row == N_META - 1, u_next, pltpu.roll(u, N_META - 1, 0))
    _conv_taps(u, u_left, u_right, b_ref[...], w_ref, bias_ref, o_ref)


def _conv_gate(z, conv_w, conv_b):
    nb = CONV_COL_BLOCKS
    halo_per_tile = CONV_ROWS // SUBLANES
    meta_halo0 = (M_REAL + N_META - SUBLANES) // SUBLANES
    meta_halo_step = N_META // SUBLANES

    def prev_block(i):
        first = i % CONV_TILES_PER_BATCH == 0
        batch = i // CONV_TILES_PER_BATCH
        return jnp.where(first, meta_halo0 + meta_halo_step * batch,
                         i * halo_per_tile - 1)

    def next_block(i):
        last = i % CONV_TILES_PER_BATCH == CONV_TILES_PER_BATCH - 1
        return jnp.where(last, i * halo_per_tile, (i + 1) * halo_per_tile)

    def tile(col0):
        return pl.BlockSpec((CONV_ROWS, CONV_COLS), lambda i, j: (i, j + col0))

    def halo(block_fn, col0):
        return pl.BlockSpec((SUBLANES, CONV_COLS),
                            lambda i, j: (block_fn(i), j + col0))

    w_spec = pl.BlockSpec((3, CONV_COLS), lambda i, j: (0, j))
    bias_spec = pl.BlockSpec((1, CONV_COLS), lambda i, j: (0, j))
    bias = conv_b.reshape(1, D_MODEL)
    y = pl.pallas_call(
        _conv_real_kernel,
        grid=(M_REAL // CONV_ROWS, nb),
        in_specs=[tile(0), tile(nb), tile(2 * nb),
                  halo(prev_block, nb), halo(prev_block, 2 * nb),
                  halo(next_block, nb), halo(next_block, 2 * nb),
                  w_spec, bias_spec],
        out_specs=pl.BlockSpec((CONV_ROWS, CONV_COLS), lambda i, j: (i, j)),
        out_shape=jax.ShapeDtypeStruct((M_ROWS, D_MODEL), BF16),
        compiler_params=_params(("parallel", "parallel")),
        name="conv_real",
    )(z, z, z, z, z, z, z, conv_w, bias)

    meta0 = M_REAL // N_META
    real0_step = SEQ // SUBLANES

    def meta_tile(col0):
        return pl.BlockSpec((N_META, CONV_COLS), lambda b, j: (meta0 + b, j + col0))

    def first_real(col0):
        return pl.BlockSpec((SUBLANES, CONV_COLS),
                            lambda b, j: (b * real0_step, j + col0))

    w_spec = pl.BlockSpec((3, CONV_COLS), lambda b, j: (0, j))
    bias_spec = pl.BlockSpec((1, CONV_COLS), lambda b, j: (0, j))
    return pl.pallas_call(
        _conv_meta_kernel,
        grid=(BATCH, nb),
        in_specs=[meta_tile(0), meta_tile(nb), meta_tile(2 * nb),
                  first_real(nb), first_real(2 * nb), w_spec, bias_spec,
                  pl.BlockSpec(memory_space=pl.ANY)],
        out_specs=pl.BlockSpec((N_META, CONV_COLS), lambda b, j: (meta0 + b, j)),
        out_shape=jax.ShapeDtypeStruct((M_ROWS, D_MODEL), BF16),
        input_output_aliases={7: 0},
        compiler_params=_params(("parallel", "parallel")),
        name="conv_meta",
    )(z, z, z, z, z, conv_w, bias, y)


def _conv_mixer(h, stream, w_in, conv_w, conv_b, w_out, layer, next_gain):
    z = _normed_matmul(stream, w_in, layer, tm=ROWS_LARGE, tn=PROJ_COLS,
                       out_dtype=F32, name="conv_in")
    y = _conv_gate(z, conv_w[layer], conv_b[layer])
    return _residual_matmul(y, w_out, layer, h, tm=ROWS_LARGE, tn=PROJ_COLS,
                            scale=1.0, next_gain=next_gain, name="conv_out")


def _rope_tables():
    t = jnp.arange(SEQ, dtype=jnp.int32)
    real_row = (t // GRID_W).astype(F32)
    real_col = (t % GRID_W).astype(F32)
    meta_row = jnp.full((N_META,), -1.0, F32)
    meta_col = jnp.arange(N_META, dtype=F32)
    row = jnp.concatenate([jnp.tile(real_row, BATCH), jnp.tile(meta_row, BATCH)])
    col = jnp.concatenate([jnp.tile(real_col, BATCH), jnp.tile(meta_col, BATCH)])
    inv_freq = ROPE_THETA ** (-jnp.arange(0, ROPE_AXIS_DIM, 2, dtype=F32)
                              / ROPE_AXIS_DIM)
    ar = row[:, None] * inv_freq
    ac = col[:, None] * inv_freq
    zero = jnp.zeros_like(ar)
    cos = jnp.concatenate([jnp.cos(ar), jnp.cos(ar), jnp.cos(ac), jnp.cos(ac)], -1)
    sin_lo = jnp.concatenate([-jnp.sin(ar), zero, -jnp.sin(ac), zero], -1)
    sin_hi = jnp.concatenate([zero, jnp.sin(ar), zero, jnp.sin(ac)], -1)
    return cos, sin_lo, sin_hi


def _qkv_kernel(x_ref, ss_ref, w_ref, cos_ref, slo_ref, shi_ref, qg_ref, kg_ref,
                o_ref):
    j = pl.program_id(1)
    y = _row_scale(ss_ref) * jnp.dot(x_ref[...], w_ref[...].astype(BF16),
                                     preferred_element_type=F32)
    half = ROPE_AXIS_DIM // 2

    @pl.when(j < (Q_WIDTH + KV_WIDTH) // PROJ_COLS)
    def _():
        cos = cos_ref[...]
        slo = slo_ref[...]
        shi = shi_ref[...]
        gain = jnp.where(j < Q_WIDTH // PROJ_COLS,
                         qg_ref[...] * (HEAD_DIM ** -0.5 * LOG2_E), kg_ref[...])
        for hd in range(PROJ_COLS // HEAD_DIM):
            sl = slice(hd * HEAD_DIM, (hd + 1) * HEAD_DIM)
            t = y[:, sl]
            ms = jnp.mean(t * t, axis=-1, keepdims=True)
            t = t * lax.rsqrt(ms + NORM_EPS) * gain
            t = (t * cos + pltpu.roll(t, HEAD_DIM - half, 1) * slo
                 + pltpu.roll(t, half, 1) * shi)
            o_ref[:, sl] = t.astype(o_ref.dtype)

    @pl.when(j >= (Q_WIDTH + KV_WIDTH) // PROJ_COLS)
    def _():
        o_ref[...] = y.astype(o_ref.dtype)


def _qkv_projection(stream, w, layer, q_gain, k_gain, *, tm, name):
    assert Q_WIDTH % PROJ_COLS == 0 and KV_WIDTH % PROJ_COLS == 0
    hg, ss = stream
    m, k = hg.shape
    cos, slo, shi = _rope_tables()
    table = pl.BlockSpec((tm, HEAD_DIM), lambda i, j: (i, 0))
    gain = pl.BlockSpec((1, HEAD_DIM), lambda i, j: (0, 0))
    return pl.pallas_call(
        _qkv_kernel,
        grid=(m // tm, QKV_WIDTH // PROJ_COLS),
        in_specs=[_panel_spec(tm, k), _ss_spec(tm), _weight_spec(layer, k, PROJ_COLS),
                  table, table, table, gain, gain],
        out_specs=pl.BlockSpec((tm, PROJ_COLS), lambda i, j: (i, j)),
        out_shape=jax.ShapeDtypeStruct((m, QKV_WIDTH), BF16),
        compiler_params=_params(("parallel", "arbitrary")),
        name=name,
    )(hg, ss, w, cos, slo, shi, q_gain.reshape(1, HEAD_DIM),
      k_gain.reshape(1, HEAD_DIM))


ATTN_KV_CHUNK = 512
ATTN_STRIP_ROWS = 32
META_PAD = 128

_NT = (((1,), (1,)), ((), ()))


def _flash_kernel(q_ref, k_ref, v_ref, km_ref, vm_ref, *rest, tq):
    (o_ref, q_all, s0, s1, p0, p1, a0, a1, m_ref, l_ref, acc_ref) = rest[-11:]
    s_buf, p_buf, a_buf = (s0, s1), (p0, p1), (a0, a1)
    nq = GROUP * tq
    strip = min(ATTN_STRIP_ROWS, tq)
    n_chunks = SEQ // ATTN_KV_CHUNK

    for g in range(GROUP):
        q_all[g * tq:(g + 1) * tq, :] = q_ref[:, g * HEAD_DIM:(g + 1) * HEAD_DIM]
    m_ref[...] = jnp.full(m_ref.shape, -jnp.inf, F32)
    l_ref[...] = jnp.zeros(l_ref.shape, F32)

    def scores(kc, buf, width):
        s_buf[buf][:, :width] = lax.dot_general(q_all[...], kc, _NT,
                                                preferred_element_type=F32)

    def softmax(buf, width, n_valid):
        for r in range(nq // strip):
            rs = slice(r * strip, (r + 1) * strip)
            s = s_buf[buf][rs, :width]
            if n_valid < width:
                col = lax.broadcasted_iota(jnp.int32, s.shape, 1)
                s = jnp.where(col < n_valid, s, -jnp.inf)
            m_old = m_ref[rs, :]
            m_new = jnp.maximum(m_old, jnp.max(s, axis=-1, keepdims=True))
            alpha = jnp.exp2(m_old - m_new)
            l_new = alpha * l_ref[rs, :]
            for j in range(width // LANES):
                cols = slice(j * LANES, (j + 1) * LANES)
                p = jnp.exp2(s[:, cols] - m_new)
                l_new = l_new + p
                p_buf[buf][rs, cols] = p.astype(BF16)
            l_ref[rs, :] = l_new
            m_ref[rs, :] = m_new
            a_buf[buf][rs, :] = alpha

    def values(vc, buf, width):
        pv = jnp.dot(p_buf[buf][:, :width], vc, preferred_element_type=F32)
        acc_ref[...] = a_buf[buf][...] * acc_ref[...] + pv

    def k_chunk(c):
        return k_ref[pl.ds(pl.multiple_of(c * ATTN_KV_CHUNK, ATTN_KV_CHUNK),
                           ATTN_KV_CHUNK), :]

    def v_chunk(c):
        return v_ref[pl.ds(pl.multiple_of(c * ATTN_KV_CHUNK, ATTN_KV_CHUNK),
                           ATTN_KV_CHUNK), :]

    pad = jnp.zeros((META_PAD - N_META, HEAD_DIM), BF16)
    scores(jnp.concatenate([km_ref[...], pad], axis=0), 1, META_PAD)
    scores(k_chunk(0), 0, ATTN_KV_CHUNK)
    softmax(1, META_PAD, N_META)
    acc_ref[...] = jnp.dot(p_buf[1][:, :META_PAD],
                           jnp.concatenate([vm_ref[...], pad], axis=0),
                           preferred_element_type=F32)
    scores(k_chunk(1), 1, ATTN_KV_CHUNK)
    softmax(0, ATTN_KV_CHUNK, ATTN_KV_CHUNK)

    def body(i, carry):
        c = 2 * i + 1
        values(v_chunk(c - 1), 0, ATTN_KV_CHUNK)
        scores(k_chunk(c + 1), 0, ATTN_KV_CHUNK)
        softmax(1, ATTN_KV_CHUNK, ATTN_KV_CHUNK)
        values(v_chunk(c), 1, ATTN_KV_CHUNK)
        scores(k_chunk(c + 2), 1, ATTN_KV_CHUNK)
        softmax(0, ATTN_KV_CHUNK, ATTN_KV_CHUNK)
        return carry

    lax.fori_loop(0, (n_chunks - 2) // 2, body, 0)
    values(v_chunk(n_chunks - 2), 0, ATTN_KV_CHUNK)
    softmax(1, ATTN_KV_CHUNK, ATTN_KV_CHUNK)
    values(v_chunk(n_chunks - 1), 1, ATTN_KV_CHUNK)

    for g in range(GROUP):
        rows = slice(g * tq, (g + 1) * tq)
        denom = jnp.sum(l_ref[rows, :], axis=-1, keepdims=True)
        o_ref[:, g * HEAD_DIM:(g + 1) * HEAD_DIM] = (
            acc_ref[rows, :] / denom).astype(o_ref.dtype)


def _flash_scratch(tq):
    nq = GROUP * tq
    scores = pltpu.VMEM((nq, ATTN_KV_CHUNK), F32)
    probs = pltpu.VMEM((nq, ATTN_KV_CHUNK), BF16)
    stat = pltpu.VMEM((nq, LANES), F32)
    return [pltpu.VMEM((nq, HEAD_DIM), BF16),
            scores, scores, probs, probs,
            stat, stat,
            stat,
            stat,
            pltpu.VMEM((nq, HEAD_DIM), F32)]


def _attention(qkv, *, tq):
    qw = GROUP * HEAD_DIM
    n_q = SEQ // tq
    meta0 = M_REAL // N_META
    k0 = Q_WIDTH // HEAD_DIM
    v0 = (Q_WIDTH + KV_WIDTH) // HEAD_DIM

    def kv_real(col0):
        return pl.BlockSpec((SEQ, HEAD_DIM), lambda b, g, i: (b, col0 + g))

    def kv_meta(col0):
        return pl.BlockSpec((N_META, HEAD_DIM), lambda b, g, i: (meta0 + b, col0 + g))

    kv_specs = [kv_real(k0), kv_real(v0), kv_meta(k0), kv_meta(v0)]
    q_spec = pl.BlockSpec((tq, qw), lambda b, g, i: (b * n_q + i, g))
    o_real = pl.pallas_call(
        functools.partial(_flash_kernel, tq=tq),
        grid=(BATCH, N_KV_HEADS, n_q),
        in_specs=[q_spec] + kv_specs,
        out_specs=q_spec,
        out_shape=jax.ShapeDtypeStruct((M_ROWS, Q_WIDTH), BF16),
        scratch_shapes=_flash_scratch(tq),
        compiler_params=_params(("parallel", "parallel", "arbitrary")),
        name="attn_real",
    )(qkv, qkv, qkv, qkv, qkv)

    qm_spec = pl.BlockSpec((N_META, qw), lambda b, g, i: (meta0 + b, g))
    return pl.pallas_call(
        functools.partial(_flash_kernel, tq=N_META),
        grid=(BATCH, N_KV_HEADS, 1),
        in_specs=[qm_spec] + kv_specs + [pl.BlockSpec(memory_space=pl.ANY)],
        out_specs=qm_spec,
        out_shape=jax.ShapeDtypeStruct((M_ROWS, Q_WIDTH), BF16),
        input_output_aliases={5: 0},
        scratch_shapes=_flash_scratch(N_META),
        compiler_params=_params(("parallel", "parallel", "arbitrary")),
        name="attn_meta",
    )(qkv, qkv, qkv, qkv, qkv, o_real)


def _attn_mixer(h, stream, w_qkv, q_gain, k_gain, w_o, layer, next_gain):
    qkv = _qkv_projection(stream, w_qkv, layer, q_gain[layer], k_gain[layer],
                          tm=ROWS_LARGE, name="attn_qkv")
    o = _attention(qkv, tq=ATTN_Q_ROWS)
    return _residual_matmul(o, w_o, layer, h, tm=ROWS_LARGE, tn=PROJ_COLS,
                            scale=1.0, next_gain=next_gain, name="attn_out")


def kernel(x, meta_tokens, ffn_a_norm, ffn_a_w_gate, ffn_a_w_up, ffn_a_w_down, ffn_b_norm, ffn_b_w_gate, ffn_b_w_up, ffn_b_w_down, conv_norm, conv_w_in, conv_w, conv_b, conv_w_out, attn_norm, attn_w_qkv, attn_q_norm, attn_k_norm, attn_w_o, final_norm):
    bsz, n_real, d = x.shape
    assert (bsz, n_real, d) == (BATCH, SEQ, D_MODEL)
    h, stream = _embed(x, meta_tokens.astype(x.dtype), ffn_a_norm[0],
                       block_rows=REAL_ROWS)

    for i in range(DEPTH):
        j = i // 2
        is_conv = i % 2 == 0
        mixer_gain = conv_norm[j] if is_conv else attn_norm[j]
        h, stream = _ffn(h, stream, ffn_a_w_gate, ffn_a_w_up, ffn_a_w_down, i,
                         mixer_gain, name=f"ffn_a{i}")
        if is_conv:
            h, stream = _conv_mixer(h, stream, conv_w_in, conv_w, conv_b,
                                    conv_w_out, j, ffn_b_norm[i])
        else:
            h, stream = _attn_mixer(h, stream, attn_w_qkv, attn_q_norm,
                                    attn_k_norm, attn_w_o, j, ffn_b_norm[i])
        after = ffn_a_norm[i + 1] if i + 1 < DEPTH else None
        h, stream = _ffn(h, stream, ffn_b_w_gate, ffn_b_w_up, ffn_b_w_down, i,
                         after, name=f"ffn_b{i}")

    out = _rmsnorm(h, final_norm, rows=M_REAL, out_dtype=F32, block_rows=REAL_ROWS,
                   name="final_norm")
    return out.reshape(BATCH, SEQ, D_MODEL)
```
